```python
import jax
import jax.numpy as jnp
from jax import lax
import numpy as np

D_MODEL = 1024
BATCH = 1
SEQ = 16384
DEPTH = 2
DEC_BATCH = 8
DEC_SEQ = 64
PAST_LEN = 4096

CHUNK = 64
Q_BLOCK = 128
EPS = 1e-6
D_A = D_MODEL
H_A = D_A // 64
BW_A = D_A // H_A
CONV_A = 4
C_LRU = 8.0
H_B = 8
DH_B = D_MODEL // H_B
D_B = H_B * DH_B
H_C = 8
DK_C = D_MODEL // H_C
DV_C = DK_C
D_CK = H_C * DK_C
D_CV = H_C * DV_C
N_BRANCH = 3
D_BR = D_MODEL
D_FF = 3 * D_MODEL
CONV_F = 3
N_IN = D_A + 3 * D_B + 2 * D_CK + 2 * D_CV + N_BRANCH * D_MODEL

kernel_name = 'hybrid_streaming_rglru_stickbreak_hgrn2'


def _rms_norm(x, g):
    x32 = x.astype(jnp.float32)
    y = x32 * lax.rsqrt(jnp.mean(x32 * x32, axis=-1, keepdims=True) + EPS)
    return (y * g.astype(jnp.float32)).astype(x.dtype)


def _split_cols(p, sizes):
    parts, start = [], 0
    for s in sizes:
        parts.append(p[..., start:start + s])
        start += s
    return parts


def _causal_dwconv(x, prev, w, b):
    width, t = w.shape[0], x.shape[1]
    xp = jnp.concatenate([prev.astype(x.dtype), x], axis=1)
    y = b + w[width - 1] * x
    for j in range(width - 1):
        y = y + w[j] * xp[:, j:j + t]
    return y, xp[:, t:]


def _lru_combine(left, right):
    a_l, b_l = left
    a_r, b_r = right
    return a_l * a_r, a_r * b_l + b_r


def _rg_lru(x, h0, gate_w, gate_b, lam):
    bsz, t, _ = x.shape
    f32 = jnp.float32
    xf = x.astype(f32)
    gates = jnp.einsum('bthi,ghij->gbthj', xf.reshape(bsz, t, H_A, BW_A), gate_w.astype(f32))
    gates = gates.reshape(2, bsz, t, D_A) + gate_b.astype(f32)[:, None, None, :]
    r, i = jax.nn.sigmoid(gates[0]), jax.nn.sigmoid(gates[1])
    log_a = -C_LRU * r * jax.nn.softplus(-lam.astype(f32))
    a = jnp.exp(log_a)
    b = jnp.sqrt(-jnp.expm1(2.0 * log_a)) * (i * xf)
    b = b.at[:, 0].add(a[:, 0] * h0.astype(f32))
    _, h = lax.associative_scan(_lru_combine, (a, b), axis=1)
    return h, h[:, -1]


def _sb_block(q, k, v, q0):
    tq, tk = q.shape[1], k.shape[1]
    z = jnp.einsum('bqhd,bkhd->bhqk', q, k, preferred_element_type=jnp.float32) * (DH_B ** -0.5)
    q_pos = q0 + jnp.arange(tq)
    k_pos = jnp.arange(tk)
    mask = k_pos[None, :] < q_pos[:, None]
    log_1mb = jnp.where(mask, -jax.nn.softplus(z), 0.0)
    between = lax.cumsum(log_1mb, axis=3, reverse=True) - log_1mb
    w = jnp.where(mask, jnp.exp(jax.nn.log_sigmoid(z) + between), 0.0)
    return jnp.einsum('bhqk,bkhd->bqhd', w, v.astype(jnp.float32))


def _stick_breaking(q, k, v):
    t = q.shape[1]
    past = k.shape[1] - t
    outs = []
    for s in range(0, t, Q_BLOCK):
        e = min(s + Q_BLOCK, t)
        outs.append(_sb_block(q[:, s:e], k[:, :past + e], v[:, :past + e], past + s))
    return jnp.concatenate(outs, axis=1)


def _gla_chunk(s0, q, k, v, g):
    t = q.shape[1]
    b = jnp.cumsum(g, axis=1)
    o = jnp.einsum('bthk,bhkv->bthv', q * jnp.exp(b), s0)
    causal = jnp.tril(jnp.ones((t, t), dtype=bool))[None, :, :, None, None]
    decay = jnp.exp(jnp.where(causal, b[:, :, None] - b[:, None, :], -jnp.inf))
    attn = jnp.einsum('bthk,bshk,btshk->bhts', q, k, decay)
    o = o + jnp.einsum('bhts,bshv->bthv', attn, v)
    b_last = b[:, -1]
    s_new = jnp.exp(b_last)[..., None] * s0 + jnp.einsum('bshk,bshv->bhkv', k * jnp.exp(b_last[:, None] - b), v)
    return s_new, o


def _hgrn2(q_raw, f_raw, i_raw, og_raw, s0, lower_bound, norm_g):
    bsz, t, _ = q_raw.shape
    f32 = jnp.float32
    q = jax.nn.silu(q_raw.astype(f32)).reshape(bsz, t, H_C, DK_C)
    lb = lower_bound.reshape(H_C, DK_C)
    log_f = jnp.logaddexp(jnp.log(lb), jnp.log1p(-lb) + jax.nn.log_sigmoid(f_raw.astype(f32).reshape(bsz, t, H_C, DK_C)))
    k = -jnp.expm1(log_f)
    v = i_raw.astype(f32).reshape(bsz, t, H_C, DV_C)
    s0 = s0.astype(f32)
    if t % CHUNK:
        s_last, o = _gla_chunk(s0, q, k, v, log_f)
    else:
        nc = t // CHUNK

        def chunks(a):
            return a.reshape(bsz, nc, CHUNK, H_C, a.shape[-1]).swapaxes(0, 1)

        s_last, o = lax.scan(lambda s, c: _gla_chunk(s, *c), s0, (chunks(q), chunks(k), chunks(v), chunks(log_f)))
        o = o.swapaxes(0, 1).reshape(bsz, t, H_C, DV_C)
    o = o * lax.rsqrt(jnp.mean(o * o, axis=-1, keepdims=True) + EPS) * norm_g.astype(f32)
    o = o * jax.nn.silu(og_raw.astype(f32)).reshape(bsz, t, H_C, DV_C)
    return o.reshape(bsz, t, D_CV), s_last


def _layer(x, conv_a_prev, h0, k_past, v_past, s0, conv_f_prev,
           norm_mix_g, w_in, conv_a_w, conv_a_b, lru_gate_w, lru_gate_b, lru_lambda,
           hgrn_norm_g, w_branch, w_out, norm_ffn_g, w_up, conv_f_w, conv_f_b, w_down, lower_bound):
    bsz, t, _ = x.shape
    dt = x.dtype
    h = _rms_norm(x, norm_mix_g)
    xa, qb, kb, vb, qc, fc, ic, gc, gm = _split_cols(
        h @ w_in, (D_A, D_B, D_B, D_B, D_CK, D_CK, D_CV, D_CV, N_BRANCH * D_MODEL))
    xa_c, conv_a_new = _causal_dwconv(xa, conv_a_prev, conv_a_w, conv_a_b)
    ya, h_last = _rg_lru(xa_c, h0, lru_gate_w, lru_gate_b, lru_lambda)
    kh = kb.reshape(bsz, t, H_B, DH_B)
    vh = vb.reshape(bsz, t, H_B, DH_B)
    k_all = kh if k_past is None else jnp.concatenate([k_past.astype(dt), kh], axis=1)
    v_all = vh if v_past is None else jnp.concatenate([v_past.astype(dt), vh], axis=1)
    yb = _stick_breaking(qb.reshape(bsz, t, H_B, DH_B), k_all, v_all).reshape(bsz, t, D_B)
    yc, s_last = _hgrn2(qc, fc, ic, gc, s0, lower_bound, hgrn_norm_g)
    ys = jnp.stack([ya.astype(dt), yb.astype(dt), yc.astype(dt)], axis=2)
    gates = jax.nn.sigmoid(gm.astype(jnp.float32)).reshape(bsz, t, N_BRANCH, D_MODEL)
    branch = jnp.einsum('btnc,ncd->btnd', ys, w_branch, preferred_element_type=jnp.float32)
    merged = jnp.sum(gates * branch, axis=2).astype(dt)
    x = x + merged @ w_out
    h = _rms_norm(x, norm_ffn_g)
    u, v = _split_cols(h @ w_up, (D_FF, D_FF))
    u_c, conv_f_new = _causal_dwconv(u, conv_f_prev, conv_f_w, conv_f_b)
    x = x + (jax.nn.gelu(u_c) * v) @ w_down
    return x, (conv_a_new, h_last.astype(dt), kh, vh, s_last.astype(dt), conv_f_new)


def setup_inputs(seed: int = 0) -> dict:
    key = jax.random.key(seed)
    ks = jax.random.split(key, 32)
    f32 = jnp.float32

    def nrm(k, shape, scale):
        return scale * jax.random.normal(k, shape, f32)

    u = jax.random.uniform(ks[0], (DEPTH, D_A), f32, 0.9, 0.999)
    p = u ** (1.0 / C_LRU)
    lru_lambda = jnp.log(p) - jnp.log1p(-p)
    return {
        'x_prompt': nrm(ks[1], (BATCH, SEQ, D_MODEL), 1.0),
        'x_sample': nrm(ks[2], (DEC_BATCH, DEC_SEQ, D_MODEL), 1.0),
        'cache_conv_a': nrm(ks[3], (DEPTH, DEC_BATCH, CONV_A - 1, D_A), 1.0),
        'state_lru': nrm(ks[4], (DEPTH, DEC_BATCH, D_A), 0.5),
        'cache_k': nrm(ks[5], (DEPTH, DEC_BATCH, PAST_LEN, H_B, DH_B), 1.0),
        'cache_v': nrm(ks[6], (DEPTH, DEC_BATCH, PAST_LEN, H_B, DH_B), 1.0),
        'state_hgrn': nrm(ks[7], (DEPTH, DEC_BATCH, H_C, DK_C, DV_C), 0.5),
        'cache_conv_ffn': nrm(ks[8], (DEPTH, DEC_BATCH, CONV_F - 1, D_FF), 1.0),
        'norm_mix_g': 1.0 + nrm(ks[9], (DEPTH, D_MODEL), 0.05),
        'w_in': nrm(ks[10], (DEPTH, D_MODEL, N_IN), D_MODEL ** -0.5),
        'conv_a_w': nrm(ks[11], (DEPTH, CONV_A, D_A), CONV_A ** -0.5),
        'conv_a_b': nrm(ks[12], (DEPTH, D_A), 0.02),
        'lru_gate_w': nrm(ks[13], (DEPTH, 2, H_A, BW_A, BW_A), BW_A ** -0.5),
        'lru_gate_b': nrm(ks[14], (DEPTH, 2, D_A), 0.02),
        'lru_lambda': lru_lambda,
        'hgrn_gamma': nrm(ks[15], (DEPTH, D_CK), 0.5),
        'hgrn_norm_g': 1.0 + nrm(ks[16], (DEPTH, DV_C), 0.05),
        'w_branch': nrm(ks[17], (DEPTH, N_BRANCH, D_BR, D_MODEL), D_BR ** -0.5),
        'w_out': nrm(ks[18], (DEPTH, D_MODEL, D_MODEL), D_MODEL ** -0.5),
        'norm_ffn_g': 1.0 + nrm(ks[19], (DEPTH, D_MODEL), 0.05),
        'w_up': nrm(ks[20], (DEPTH, D_MODEL, 2 * D_FF), D_MODEL ** -0.5),
        'conv_f_w': nrm(ks[21], (DEPTH, CONV_F, D_FF), CONV_F ** -0.5),
        'conv_f_b': nrm(ks[22], (DEPTH, D_FF), 0.02),
        'w_down': nrm(ks[23], (DEPTH, D_FF, D_MODEL), D_FF ** -0.5),
        'norm_final_g': 1.0 + nrm(ks[24], (D_MODEL,), 0.05),
    }


def reference(x_prompt, x_sample, cache_conv_a, state_lru, cache_k, cache_v, state_hgrn, cache_conv_ffn,
              norm_mix_g, w_in, conv_a_w, conv_a_b, lru_gate_w, lru_gate_b, lru_lambda, hgrn_gamma,
              hgrn_norm_g, w_branch, w_out, norm_ffn_g, w_up, conv_f_w, conv_f_b, w_down, norm_final_g):
    cum = jnp.cumsum(jax.nn.softmax(hgrn_gamma.astype(jnp.float32), axis=0), axis=0)
    lower_bounds = cum - cum[0:1]

    def params(l):
        return (norm_mix_g[l], w_in[l], conv_a_w[l], conv_a_b[l], lru_gate_w[l], lru_gate_b[l], lru_lambda[l],
                hgrn_norm_g[l], w_branch[l], w_out[l], norm_ffn_g[l], w_up[l], conv_f_w[l], conv_f_b[l],
                w_down[l], lower_bounds[l])

    bp, dt = x_prompt.shape[0], x_prompt.dtype
    xp = x_prompt
    prompt_states = []
    for l in range(DEPTH):
        xp, st = _layer(xp, jnp.zeros((bp, CONV_A - 1, D_A), dt), jnp.zeros((bp, D_A), jnp.float32), None, None,
                        jnp.zeros((bp, H_C, DK_C, DV_C), jnp.float32), jnp.zeros((bp, CONV_F - 1, D_FF), dt),
                        *params(l))
        prompt_states.append(st)
    y_prompt = _rms_norm(xp, norm_final_g)

    xs = x_sample
    sample_states = []
    for l in range(DEPTH):
        xs, st = _layer(xs, cache_conv_a[l], state_lru[l], cache_k[l], cache_v[l], state_hgrn[l],
                        cache_conv_ffn[l], *params(l))
        sample_states.append(st)
    y_sample = _rms_norm(xs, norm_final_g)

    p_conv_a, p_lru, p_k, p_v, p_hgrn, p_conv_ffn = (jnp.stack(z) for z in zip(*prompt_states))
    s_conv_a, s_lru, s_k, s_v, s_hgrn, s_conv_ffn = (jnp.stack(z) for z in zip(*sample_states))
    return (y_prompt, y_sample, p_conv_a, s_conv_a, p_lru, s_lru, p_k, s_k, p_v, s_v,
            p_hgrn, s_hgrn, p_conv_ffn, s_conv_ffn)
```

```python
import functools

import numpy as np
import jax
import jax.numpy as jnp
from jax import lax
from jax.experimental import pallas as pl
from jax.experimental.pallas import tpu as pltpu

F32 = jnp.float32
BF16 = jnp.bfloat16

EPS = 1e-6
LANES = 128
SUBLANES = 8
VMEM_LIMIT = 48 * 1024 * 1024

N_HEADS = 8
LRU_GROUP = 64
C_LRU = 8.0
CONV_A = 4
CONV_F = 3
N_BRANCH = 3
HGRN_CHUNK = 64
HGRN_LEVELS = 6
ATT_BLOCK = 256


def _cparams(semantics):
    return pltpu.CompilerParams(dimension_semantics=semantics, vmem_limit_bytes=VMEM_LIMIT)


def _sigmoid(x):
    return 1.0 / (1.0 + jnp.exp(-x))


def _softplus(x):
    return jnp.maximum(x, 0.0) + jnp.log1p(jnp.exp(-jnp.abs(x)))


def _rms_norm(x, g):
    ms = jnp.mean(x * x, axis=-1, keepdims=True)
    return x * lax.rsqrt(ms + EPS) * g


def _dot(a, b):
    return jnp.dot(a, b, preferred_element_type=F32)


def _dot_nt(a, b):
    return lax.dot_general(a, b, (((1,), (1,)), ((), ())), preferred_element_type=F32)


def _dot_tn(a, b):
    return lax.dot_general(a, b, (((0,), (0,)), ((), ())), preferred_element_type=F32)


def _shift_rows(x, tail, s):
    n_rows = x.shape[0]
    rows = lax.broadcasted_iota(jnp.int32, (SUBLANES, x.shape[1]), 0)
    rolled = pltpu.roll(x, s, 0)
    first = jnp.where(rows < s, pltpu.roll(tail, s, 0), rolled[0:SUBLANES])
    if n_rows == SUBLANES:
        return first
    return jnp.concatenate([first, rolled[SUBLANES:]], axis=0)


def _proj_in_body(x_ref, g_ref, w_ref, cs_ref, p_ref, qkv_ref, h_scr, *, jq0, nq):
    j = pl.program_id(1)

    @pl.when(j == 0)
    def _():
        h_scr[...] = _rms_norm(x_ref[...], g_ref[...]).astype(BF16)

    acc = _dot(h_scr[...], w_ref[...])
    p_ref[...] = acc

    @pl.when(jnp.logical_and(j >= jq0, j < jq0 + nq))
    def _():
        qkv_ref[...] = (acc * cs_ref[...]).astype(BF16)


def _proj_in(x2d, g, w_bf16, col_scale, d_model):
    m, d = x2d.shape
    n_in = w_bf16.shape[1]
    tm = min(1024, m)
    tn = d_model
    jq0, nq = 1, 3
    body = functools.partial(_proj_in_body, jq0=jq0, nq=nq)
    return pl.pallas_call(
        body,
        grid=(m // tm, n_in // tn),
        in_specs=[
            pl.BlockSpec((tm, d), lambda i, j: (i, 0)),
            pl.BlockSpec((1, d), lambda i, j: (0, 0)),
            pl.BlockSpec((d, tn), lambda i, j: (0, j)),
            pl.BlockSpec((1, tn), lambda i, j: (0, jnp.clip(j - jq0, 0, nq - 1))),
        ],
        out_specs=[
            pl.BlockSpec((tm, tn), lambda i, j: (i, j)),
            pl.BlockSpec((tm, tn), lambda i, j: (i, jnp.clip(j - jq0, 0, nq - 1))),
        ],
        out_shape=[
            jax.ShapeDtypeStruct((m, n_in), F32),
            jax.ShapeDtypeStruct((m, nq * tn), BF16),
        ],
        scratch_shapes=[pltpu.VMEM((tm, d), BF16)],
        compiler_params=_cparams(("arbitrary", "arbitrary")),
        name="proj_in",
    )(x2d, g, w_bf16, col_scale)


def _lru_body(x_ref, prev_ref, h0_ref, cw_ref, cb_ref, wg_ref, gb_ref, lam_ref,
              ya_ref, convnew_ref, hlast_ref, tail_scr, h_scr, *, tt):
    t = pl.program_id(2)
    nt = pl.num_programs(2)

    @pl.when(t == 0)
    def _():
        tail_scr[...] = jnp.zeros_like(tail_scr)
        tail_scr[SUBLANES - (CONV_A - 1):SUBLANES, :] = prev_ref[0]
        h_scr[...] = h0_ref[0]

    x = x_ref[0]
    tail = tail_scr[...]
    y = cb_ref[...] + cw_ref[CONV_A - 1:CONV_A, :] * x
    for s in range(1, CONV_A):
        y = y + cw_ref[CONV_A - 1 - s:CONV_A - s, :] * _shift_rows(x, tail, s)
    tail_scr[...] = x_ref[0, tt - SUBLANES:tt, :]

    gates = _dot(y.astype(BF16), wg_ref[0]) + gb_ref[0]
    r = _sigmoid(gates[:, :LANES])
    i = _sigmoid(gates[:, LANES:])
    log_a = (-C_LRU) * r * _softplus(-lam_ref[...])
    a = jnp.exp(log_a)
    th = jnp.tanh(log_a)
    b = jnp.sqrt(-2.0 * th / (1.0 - th)) * (i * y)

    row = lax.broadcasted_iota(jnp.int32, (tt, LANES), 0)
    b = b + jnp.where(row == 0, a * h_scr[...], 0.0)
    s = 1
    while s < tt:
        if s < SUBLANES:
            a_sh = jnp.where(row < s, 1.0, pltpu.roll(a, s, 0))
            b_sh = jnp.where(row < s, 0.0, pltpu.roll(b, s, 0))
        else:
            a_sh = jnp.concatenate([jnp.ones((s, LANES), F32), a[:tt - s]], axis=0)
            b_sh = jnp.concatenate([jnp.zeros((s, LANES), F32), b[:tt - s]], axis=0)
        b = a * b_sh + b
        a = a * a_sh
        s *= 2
    h_last = b[tt - 1:tt]
    ya_ref[0] = b.astype(BF16)
    h_scr[...] = h_last

    @pl.when(t == nt - 1)
    def _():
        convnew_ref[0] = x_ref[0, tt - (CONV_A - 1):tt, :]
        hlast_ref[0] = h_last


def _lru(p3d, conv_prev, h0, conv_w, conv_b, wg, gb, lam):
    bsz, t_len, _ = p3d.shape
    d_a = conv_w.shape[1]
    nc = d_a // LANES
    tt = min(512, t_len)
    body = functools.partial(_lru_body, tt=tt)
    return pl.pallas_call(
        body,
        grid=(bsz, nc, t_len // tt),
        in_specs=[
            pl.BlockSpec((1, tt, LANES), lambda b, c, t: (b, t, c)),
            pl.BlockSpec((1, CONV_A - 1, LANES), lambda b, c, t: (b, 0, c)),
            pl.BlockSpec((1, 1, LANES), lambda b, c, t: (b, 0, c)),
            pl.BlockSpec((CONV_A, LANES), lambda b, c, t: (0, c)),
            pl.BlockSpec((1, LANES), lambda b, c, t: (0, c)),
            pl.BlockSpec((1, LANES, 2 * LANES), lambda b, c, t: (c, 0, 0)),
            pl.BlockSpec((1, 1, 2 * LANES), lambda b, c, t: (c, 0, 0)),
            pl.BlockSpec((1, LANES), lambda b, c, t: (0, c)),
        ],
        out_specs=[
            pl.BlockSpec((1, tt, LANES), lambda b, c, t: (b, t, c)),
            pl.BlockSpec((1, CONV_A - 1, LANES), lambda b, c, t: (b, 0, c)),
            pl.BlockSpec((1, 1, LANES), lambda b, c, t: (b, 0, c)),
        ],
        out_shape=[
            jax.ShapeDtypeStruct((bsz, t_len, d_a), BF16),
            jax.ShapeDtypeStruct((bsz, CONV_A - 1, d_a), F32),
            jax.ShapeDtypeStruct((bsz, 1, d_a), F32),
        ],
        scratch_shapes=[pltpu.VMEM((SUBLANES, LANES), F32), pltpu.VMEM((1, LANES), F32)],
        compiler_params=_cparams(("arbitrary", "arbitrary", "arbitrary")),
        name="lru",
    )(p3d, conv_prev, h0, conv_w, conv_b, wg, gb, lam)


def _suffix_sum(l, u):
    hi = l.astype(BF16)
    lo = (l - hi.astype(F32)).astype(BF16)
    return _dot(hi, u) + _dot(lo, u)


def _attn_block(q, kb, vb, u, acc, carry, mask):
    z = _dot_nt(q, kb)
    sp = _softplus(z)
    l = -sp if mask is None else jnp.where(mask, -sp, 0.0)
    between = _suffix_sum(l, u) + carry
    w = jnp.exp(z - sp + between)
    if mask is not None:
        w = jnp.where(mask, w, 0.0)
    acc = acc + _dot(w.astype(BF16), vb)
    carry = carry + jnp.sum(l, axis=1, keepdims=True)
    return acc, carry


def _attn_body(*refs, tq, tkp, n_past):
    if n_past:
        q_ref, k_ref, v_ref, u_ref, kp_ref, vp_ref, up_ref, o_ref = refs
    else:
        q_ref, k_ref, v_ref, u_ref, o_ref = refs
    i = pl.program_id(2)
    q = q_ref[0]
    u = u_ref[...]
    row = lax.broadcasted_iota(jnp.int32, (tq, tq), 0)
    col = lax.broadcasted_iota(jnp.int32, (tq, tq), 1)

    def new_block(j):
        start = pl.multiple_of(j * tq, tq)
        return k_ref[0, pl.ds(start, tq), :], v_ref[0, pl.ds(start, tq), :]

    kd, vd = new_block(i)
    acc, carry = _attn_block(q, kd, vd, u, jnp.zeros((tq, LANES), F32), jnp.zeros((tq, 1), F32), col < row)

    def new_step(jj, state):
        kb, vb = new_block(i - 1 - jj)
        return _attn_block(q, kb, vb, u, *state, None)

    acc, carry = lax.fori_loop(0, i, new_step, (acc, carry))

    if n_past:
        up = up_ref[...]

        def past_step(jj, state):
            start = pl.multiple_of((n_past - 1 - jj) * tkp, tkp)
            kb = kp_ref[0, pl.ds(start, tkp), :].astype(BF16)
            vb = vp_ref[0, pl.ds(start, tkp), :].astype(BF16)
            return _attn_block(q, kb, vb, up, *state, None)

        acc, carry = lax.fori_loop(0, n_past, past_step, (acc, carry))

    o_ref[0] = acc.astype(BF16)


def _strict_upper(n):
    return jnp.asarray(np.triu(np.ones((n, n), np.float32), 1).T, dtype=BF16)


def _attn(qkv3d, k_past, v_past):
    bsz, t_len, d3 = qkv3d.shape
    d_b = d3 // 3
    nh = d_b // LANES
    tq = min(ATT_BLOCK, t_len)
    args = [qkv3d, qkv3d, qkv3d, _strict_upper(tq)]
    in_specs = [
        pl.BlockSpec((1, tq, LANES), lambda b, h, i: (b, i, h)),
        pl.BlockSpec((1, t_len, LANES), lambda b, h, i: (b, 0, nh + h)),
        pl.BlockSpec((1, t_len, LANES), lambda b, h, i: (b, 0, 2 * nh + h)),
        pl.BlockSpec((tq, tq), lambda b, h, i: (0, 0)),
    ]
    tkp, n_past = 0, 0
    if k_past is not None:
        past = k_past.shape[1]
        tkp = min(ATT_BLOCK, past)
        n_past = past // tkp
        args += [k_past, v_past, _strict_upper(tkp)]
        in_specs += [
            pl.BlockSpec((1, past, LANES), lambda b, h, i: (b, 0, h)),
            pl.BlockSpec((1, past, LANES), lambda b, h, i: (b, 0, h)),
            pl.BlockSpec((tkp, tkp), lambda b, h, i: (0, 0)),
        ]
    body = functools.partial(_attn_body, tq=tq, tkp=tkp, n_past=n_past)
    return pl.pallas_call(
        body,
        grid=(bsz, nh, t_len // tq),
        in_specs=in_specs,
        out_specs=pl.BlockSpec((1, tq, LANES), lambda b, h, i: (b, i, h)),
        out_shape=jax.ShapeDtypeStruct((bsz, t_len, d_b), BF16),
        compiler_params=_cparams(("arbitrary", "arbitrary", "arbitrary")),
        name="attn",
    )(*args)


def _hgrn_tables():
    c = HGRN_CHUNK
    n_seg = 2 + HGRN_LEVELS
    d = np.zeros((n_seg * c, c), np.float32)
    jj = np.arange(c)
    late = np.zeros((HGRN_LEVELS, c, LANES), np.float32)
    early = np.zeros((HGRN_LEVELS, c, LANES), np.float32)
    amask = np.zeros((HGRN_LEVELS + 1, c, c), np.float32)
    amask[HGRN_LEVELS] = np.eye(c)
    for r in range(c):
        d[r] = jj <= r
        d[c + r] = jj > r
    for li in range(HGRN_LEVELS):
        m = c >> (li + 1)
        for r in range(c):
            blk, pos = divmod(r, 2 * m)
            boundary = blk * 2 * m + m - 1
            if pos >= m:
                d[(2 + li) * c + r] = (jj > boundary) & (jj <= r)
                late[li, r] = 1.0
            else:
                d[(2 + li) * c + r] = (jj > r) & (jj <= boundary)
                early[li, r] = 1.0
        blk_id = jj // (2 * m)
        amask[li] = blk_id[:, None] == blk_id[None, :]
    d3 = np.concatenate([d, d, d], axis=1)
    return (jnp.asarray(d3, dtype=BF16), jnp.asarray(late), jnp.asarray(early), jnp.asarray(amask))


def _hgrn_body(q_ref, f_ref, i_ref, g_ref, s0_ref, gamma_ref, ng_ref, d3_ref, late_ref, early_ref, amask_ref,
               y_ref, slast_ref, st_scr, *, tt, layer):
    t = pl.program_id(2)
    nt = pl.num_programs(2)
    c = HGRN_CHUNK

    @pl.when(t == 0)
    def _():
        st_scr[...] = s0_ref[0, 0].T

    gam = gamma_ref[...]
    e = jnp.exp(gam - jnp.max(gam, axis=0, keepdims=True))
    lb = jnp.sum(e[1:layer + 1], axis=0, keepdims=True) / jnp.sum(e, axis=0, keepdims=True) \
        if layer > 0 else jnp.zeros((1, LANES), F32)
    log_lb = jnp.log(lb)
    log_1mlb = jnp.log1p(-lb)
    ng = ng_ref[...]
    d3 = d3_ref[...]

    def chunk(ci, carry):
        r0 = pl.multiple_of(ci * c, c)
        qr = q_ref[0, pl.ds(r0, c), :]
        fr = f_ref[0, pl.ds(r0, c), :]
        v = i_ref[0, pl.ds(r0, c), :]
        gr = g_ref[0, pl.ds(r0, c), :]
        q = qr * _sigmoid(qr)
        ls = jnp.minimum(fr, 0.0) - jnp.log1p(jnp.exp(-jnp.abs(fr)))
        x2 = log_1mlb + ls
        mx = jnp.maximum(log_lb, x2)
        mn = jnp.minimum(log_lb, x2)
        lf = mx + jnp.log1p(jnp.exp(mn - mx))
        kk = (1.0 - lb) * _sigmoid(-fr)

        hi = lf.astype(BF16)
        r1 = lf - hi.astype(F32)
        mid = r1.astype(BF16)
        lo = (r1 - mid.astype(F32)).astype(BF16)
        ex = jnp.exp(_dot(d3, jnp.concatenate([hi, mid, lo], axis=0)))
        xb = ex[0:c]
        xr = ex[c:2 * c]

        st = st_scr[...]
        o = _dot_nt((q * xb).astype(BF16), st.astype(BF16))
        att = amask_ref[HGRN_LEVELS] * _dot_nt(q.astype(BF16), kk.astype(BF16))
        for li in range(HGRN_LEVELS):
            xm = ex[(2 + li) * c:(3 + li) * c]
            qm = (q * xm * late_ref[li]).astype(BF16)
            km = (kk * xm * early_ref[li]).astype(BF16)
            att = att + amask_ref[li] * _dot_nt(qm, km)
        vb = v.astype(BF16)
        o = o + _dot(att.astype(BF16), vb)
        st_scr[...] = st * xb[c - 1:c, :] + _dot_tn(vb, (kk * xr).astype(BF16))

        on = _rms_norm(o, ng)
        y_ref[0, pl.ds(r0, c), :] = (on * (gr * _sigmoid(gr))).astype(BF16)
        return carry

    lax.fori_loop(0, tt // c, chunk, 0)

    @pl.when(t == nt - 1)
    def _():
        slast_ref[0, 0] = st_scr[...].T


def _hgrn(p3d, s0, gamma, norm_g, layer, col0):
    bsz, t_len, _ = p3d.shape
    nh = N_HEADS
    depth = gamma.shape[0]
    tt = min(512, t_len)
    c = HGRN_CHUNK
    d3, late, early, amask = _hgrn_tables()
    body = functools.partial(_hgrn_body, tt=tt, layer=layer)

    def col_spec(off):
        return pl.BlockSpec((1, tt, LANES), lambda b, h, t: (b, t, col0 + off * nh + h))

    def const_spec(shape):
        return pl.BlockSpec(shape, lambda b, h, t: (0,) * len(shape))

    return pl.pallas_call(
        body,
        grid=(bsz, nh, t_len // tt),
        in_specs=[
            col_spec(0), col_spec(1), col_spec(2), col_spec(3),
            pl.BlockSpec((1, 1, LANES, LANES), lambda b, h, t: (b, h, 0, 0)),
            pl.BlockSpec((depth, LANES), lambda b, h, t: (0, h)),
            const_spec((1, LANES)),
            const_spec(d3.shape), const_spec(late.shape), const_spec(early.shape), const_spec(amask.shape),
        ],
        out_specs=[
            pl.BlockSpec((1, tt, LANES), lambda b, h, t: (b, t, h)),
            pl.BlockSpec((1, 1, LANES, LANES), lambda b, h, t: (b, h, 0, 0)),
        ],
        out_shape=[
            jax.ShapeDtypeStruct((bsz, t_len, nh * LANES), BF16),
            jax.ShapeDtypeStruct((bsz, nh, LANES, LANES), F32),
        ],
        scratch_shapes=[pltpu.VMEM((LANES, LANES), F32)],
        compiler_params=_cparams(("arbitrary", "arbitrary", "arbitrary")),
        name="hgrn",
    )(p3d, p3d, p3d, p3d, s0, gamma, norm_g, d3, late, early, amask)


def _merge_body(x_ref, ya_ref, yb_ref, yc_ref, ga_ref, gb_ref, gc_ref, wb_ref, wo_ref, o_ref):
    merged = _sigmoid(ga_ref[...]) * _dot(ya_ref[...], wb_ref[0])
    merged = merged + _sigmoid(gb_ref[...]) * _dot(yb_ref[...], wb_ref[1])
    merged = merged + _sigmoid(gc_ref[...]) * _dot(yc_ref[...], wb_ref[2])
    o_ref[...] = x_ref[...] + _dot(merged.astype(BF16), wo_ref[...])


def _merge(x2d, ya, yb, yc, p2d, w_branch, w_out, gcol0):
    m, d = x2d.shape
    tm = min(512, m)
    row = lambda i: (i, 0)
    return pl.pallas_call(
        _merge_body,
        grid=(m // tm,),
        in_specs=[
            pl.BlockSpec((tm, d), row),
            pl.BlockSpec((tm, d), row), pl.BlockSpec((tm, d), row), pl.BlockSpec((tm, d), row),
            pl.BlockSpec((tm, d), lambda i: (i, gcol0)),
            pl.BlockSpec((tm, d), lambda i: (i, gcol0 + 1)),
            pl.BlockSpec((tm, d), lambda i: (i, gcol0 + 2)),
            pl.BlockSpec((N_BRANCH, d, d), lambda i: (0, 0, 0), pipeline_mode=pl.Buffered(1)),
            pl.BlockSpec((d, d), lambda i: (0, 0), pipeline_mode=pl.Buffered(1)),
        ],
        out_specs=pl.BlockSpec((tm, d), row),
        out_shape=jax.ShapeDtypeStruct((m, d), F32),
        compiler_params=_cparams(("arbitrary",)),
        name="merge",
    )(x2d, ya, yb, yc, p2d, p2d, p2d, w_branch, w_out)


def _gelu_tanh(x):
    return 0.5 * x * (1.0 + jnp.tanh(np.sqrt(2.0 / np.pi).astype(np.float32) * (x + 0.044715 * (x * x * x))))


def _ffn_body(*refs, tm, final):
    if final:
        (x_ref, g_ref, wu_ref, wv_ref, wd_ref, cw_ref, cb_ref, prev_ref, gf_ref,
         o_ref, convnew_ref, y_ref, h_scr, acc_scr, tail_scr) = refs
    else:
        (x_ref, g_ref, wu_ref, wv_ref, wd_ref, cw_ref, cb_ref, prev_ref,
         o_ref, convnew_ref, h_scr, acc_scr, tail_scr) = refs
    t = pl.program_id(1)
    k = pl.program_id(2)
    nk = pl.num_programs(2)

    @pl.when(k == 0)
    def _():
        h_scr[...] = _rms_norm(x_ref[0], g_ref[...]).astype(BF16)
        acc_scr[...] = jnp.zeros_like(acc_scr)

    @pl.when(t == 0)
    def _():
        tail_scr[k] = jnp.zeros(tail_scr.shape[1:], F32)
        tail_scr[k, SUBLANES - (CONV_F - 1):SUBLANES, :] = prev_ref[0]

    h = h_scr[...]
    u = _dot(h, wu_ref[...])
    v = _dot(h, wv_ref[...])
    tail = tail_scr[k]
    uc = cb_ref[...] + cw_ref[CONV_F - 1:CONV_F, :] * u
    for s in range(1, CONV_F):
        uc = uc + cw_ref[CONV_F - 1 - s:CONV_F - s, :] * _shift_rows(u, tail, s)
    tail_scr[k] = u[tm - SUBLANES:tm]
    convnew_ref[0] = u[tm - (CONV_F - 1):tm]
    acc_scr[...] += _dot((_gelu_tanh(uc) * v).astype(BF16), wd_ref[...])

    @pl.when(k == nk - 1)
    def _():
        x_new = x_ref[0] + acc_scr[...]
        o_ref[0] = x_new
        if final:
            y_ref[0] = _rms_norm(x_new, gf_ref[...])


def _ffn(x3d, g, w_up, w_down, conv_w, conv_b, conv_prev, final_g):
    bsz, t_len, d = x3d.shape
    d_ff = w_down.shape[0]
    tm = min(512, t_len)
    kc = 512
    nk = d_ff // kc
    final = final_g is not None
    body = functools.partial(_ffn_body, tm=tm, final=final)
    xspec = pl.BlockSpec((1, tm, d), lambda b, t, k: (b, t, 0))
    in_specs = [
        xspec,
        pl.BlockSpec((1, d), lambda b, t, k: (0, 0)),
        pl.BlockSpec((d, kc), lambda b, t, k: (0, k)),
        pl.BlockSpec((d, kc), lambda b, t, k: (0, nk + k)),
        pl.BlockSpec((kc, d), lambda b, t, k: (k, 0)),
        pl.BlockSpec((CONV_F, kc), lambda b, t, k: (0, k)),
        pl.BlockSpec((1, kc), lambda b, t, k: (0, k)),
        pl.BlockSpec((1, CONV_F - 1, kc), lambda b, t, k: (b, 0, k)),
    ]
    args = [x3d, g, w_up, w_up, w_down, conv_w, conv_b, conv_prev]
    out_specs = [xspec, pl.BlockSpec((1, CONV_F - 1, kc), lambda b, t, k: (b, 0, k))]
    out_shape = [jax.ShapeDtypeStruct((bsz, t_len, d), F32), jax.ShapeDtypeStruct((bsz, CONV_F - 1, d_ff), F32)]
    if final:
        in_specs.append(pl.BlockSpec((1, d), lambda b, t, k: (0, 0)))
        args.append(final_g)
        out_specs.append(xspec)
        out_shape.append(jax.ShapeDtypeStruct((bsz, t_len, d), F32))
    return pl.pallas_call(
        body,
        grid=(bsz, t_len // tm, nk),
        in_specs=in_specs,
        out_specs=out_specs,
        out_shape=out_shape,
        scratch_shapes=[pltpu.VMEM((tm, d), BF16), pltpu.VMEM((tm, d), F32), pltpu.VMEM((nk, SUBLANES, kc), F32)],
        compiler_params=_cparams(("arbitrary", "arbitrary", "arbitrary")),
        name="ffn",
    )(*args)


def _lru_gate_weights(gate_w, gate_b):
    n_groups = gate_w.shape[1]
    per = LANES // LRU_GROUP
    nc = n_groups // per
    w = gate_w.reshape(2, nc, per, LRU_GROUP, LRU_GROUP)
    eye = jnp.eye(per, dtype=gate_w.dtype)
    wg = jnp.einsum('gcpij,pq->cpigqj', w, eye).reshape(nc, LANES, 2 * LANES)
    gb = gate_b.reshape(2, nc, LANES).transpose(1, 0, 2).reshape(nc, 1, 2 * LANES)
    return wg.astype(BF16), gb


def _layer(x3d, conv_a_prev, h0, k_past, v_past, s0, conv_f_prev, layer, final_g,
           norm_mix_g, w_in, conv_a_w, conv_a_b, lru_gate_w, lru_gate_b, lru_lambda, hgrn_gamma,
           hgrn_norm_g, w_branch, w_out, norm_ffn_g, w_up, conv_f_w, conv_f_b, w_down):
    bsz, t_len, d = x3d.shape
    m = bsz * t_len
    x2d = x3d.reshape(m, d)
    d_b = N_HEADS * LANES
    col_scale = jnp.concatenate(
        [jnp.full((1, d_b), LANES ** -0.5, F32), jnp.ones((1, 2 * d_b), F32)], axis=1)
    p2d, qkv = _proj_in(x2d, norm_mix_g[layer][None], w_in[layer].astype(BF16), col_scale, d)
    p3d = p2d.reshape(bsz, t_len, -1)

    wg, gb = _lru_gate_weights(lru_gate_w[layer], lru_gate_b[layer])
    ya, conv_a_new, h_last = _lru(p3d, conv_a_prev, h0[:, None, :], conv_a_w[layer], conv_a_b[layer][None],
                                  wg, gb, lru_lambda[layer][None])

    if k_past is not None:
        k_past = k_past.reshape(bsz, k_past.shape[1], d_b)
        v_past = v_past.reshape(bsz, v_past.shape[1], d_b)
    yb = _attn(qkv.reshape(bsz, t_len, 3 * d_b), k_past, v_past)

    hgrn_col0 = (d + 3 * d_b) // LANES
    yc, s_last = _hgrn(p3d, s0, hgrn_gamma, hgrn_norm_g[layer][None], layer, hgrn_col0)

    gate_col0 = (p2d.shape[1] - N_BRANCH * d) // d
    x1 = _merge(x2d, ya.reshape(m, d), yb.reshape(m, d), yc.reshape(m, d), p2d,
                w_branch[layer].astype(BF16), w_out[layer].astype(BF16), gate_col0)

    outs = _ffn(x1.reshape(bsz, t_len, d), norm_ffn_g[layer][None], w_up[layer].astype(BF16),
                w_down[layer].astype(BF16), conv_f_w[layer], conv_f_b[layer][None], conv_f_prev, final_g)
    x_new, conv_f_new = outs[0], outs[1]
    y = outs[2] if final_g is not None else None

    kh = p3d[:, :, d + d_b:d + 2 * d_b].reshape(bsz, t_len, N_HEADS, LANES)
    vh = p3d[:, :, d + 2 * d_b:d + 3 * d_b].reshape(bsz, t_len, N_HEADS, LANES)
    return x_new, y, (conv_a_new, h_last[:, 0, :], kh, vh, s_last, conv_f_new)


def kernel(x_prompt, x_sample, cache_conv_a, state_lru, cache_k, cache_v, state_hgrn, cache_conv_ffn, norm_mix_g, w_in, conv_a_w, conv_a_b, lru_gate_w, lru_gate_b, lru_lambda, hgrn_gamma, hgrn_norm_g, w_branch, w_out, norm_ffn_g, w_up, conv_f_w, conv_f_b, w_down, norm_final_g):
    depth = w_in.shape[0]
    d = x_prompt.shape[-1]
    d_ff = w_down.shape[1]
    assert d == N_HEADS * LANES and w_down.shape[1] == 3 * d and w_down.shape[2] == d
    params = (norm_mix_g, w_in, conv_a_w, conv_a_b, lru_gate_w, lru_gate_b, lru_lambda, hgrn_gamma,
              hgrn_norm_g, w_branch, w_out, norm_ffn_g, w_up, conv_f_w, conv_f_b, w_down)
    final_g = norm_final_g[None]

    bp = x_prompt.shape[0]
    xp = x_prompt
    prompt_states = []
    for l in range(depth):
        xp, y_prompt, st = _layer(
            xp, jnp.zeros((bp, CONV_A - 1, d), F32), jnp.zeros((bp, d), F32), None, None,
            jnp.zeros((bp, N_HEADS, LANES, LANES), F32), jnp.zeros((bp, CONV_F - 1, d_ff), F32),
            l, final_g if l == depth - 1 else None, *params)
        prompt_states.append(st)

    xs = x_sample
    sample_states = []
    for l in range(depth):
        xs, y_sample, st = _layer(
            xs, cache_conv_a[l], state_lru[l], cache_k[l], cache_v[l], state_hgrn[l], cache_conv_ffn[l],
            l, final_g if l == depth - 1 else None, *params)
        sample_states.append(st)

    p_conv_a, p_lru, p_k, p_v, p_hgrn, p_conv_ffn = (jnp.stack(z) for z in zip(*prompt_states))
    s_conv_a, s_lru, s_k, s_v, s_hgrn, s_conv_ffn = (jnp.stack(z) for z in zip(*sample_states))
    return (y_prompt, y_sample, p_conv_a, s_conv_a, p_lru, s_lru, p_k, s_k, p_v, s_v,
            p_hgrn, s_hgrn, p_conv_ffn, s_conv_ffn)
```

```python
import functools

import numpy as np
import jax
import jax.numpy as jnp
from jax import lax
from jax.experimental import pallas as pl
from jax.experimental.pallas import tpu as pltpu

F32 = jnp.float32
BF16 = jnp.bfloat16

EPS = 1e-6
LANES = 128
SUBLANES = 8
VMEM_LIMIT = 48 * 1024 * 1024

N_HEADS = 8
LRU_GROUP = 64
C_LRU = 8.0
CONV_A = 4
CONV_F = 3
N_BRANCH = 3
HGRN_CHUNK = 64
HGRN_LEVELS = 6
ATT_BLOCK = 128
ATT_STREAMS = 8
ATT_LOG_CUTOFF = -104.0
ATT_LOG_DEAD = -1e30


def _cparams(semantics):
    return pltpu.CompilerParams(dimension_semantics=semantics, vmem_limit_bytes=VMEM_LIMIT)


def _sigmoid(x):
    return 1.0 / (1.0 + jnp.exp(-x))


def _softplus(x):
    return jnp.maximum(x, 0.0) + jnp.log1p(jnp.exp(-jnp.abs(x)))


def _rms_norm(x, g):
    ms = jnp.mean(x * x, axis=-1, keepdims=True)
    return x * lax.rsqrt(ms + EPS) * g


def _dot(a, b):
    return jnp.dot(a, b, preferred_element_type=F32)


def _dot_nt(a, b):
    return lax.dot_general(a, b, (((1,), (1,)), ((), ())), preferred_element_type=F32)


def _dot_tn(a, b):
    return lax.dot_general(a, b, (((0,), (0,)), ((), ())), preferred_element_type=F32)


def _shift_rows(x, tail, s):
    n_rows = x.shape[0]
    rows = lax.broadcasted_iota(jnp.int32, (SUBLANES, x.shape[1]), 0)
    rolled = pltpu.roll(x, s, 0)
    first = jnp.where(rows < s, pltpu.roll(tail, s, 0), rolled[0:SUBLANES])
    if n_rows == SUBLANES:
        return first
    return jnp.concatenate([first, rolled[SUBLANES:]], axis=0)


N_P_TILES = 8
N_QKV_TILES = 3


def _proj_in_body(*refs, aliased):
    if aliased:
        x_ref, g_ref, w_ref, cs_ref, _, _, p_ref, k_ref, v_ref, qkv_ref, h_scr = refs
    else:
        x_ref, g_ref, w_ref, cs_ref, p_ref, k_ref, v_ref, qkv_ref, h_scr = refs
    j = pl.program_id(1)

    @pl.when(j == 0)
    def _():
        h_scr[...] = _rms_norm(x_ref[...], g_ref[...]).astype(BF16)

    acc = _dot(h_scr[...], w_ref[...])

    @pl.when(j < N_P_TILES)
    def _():
        p_ref[...] = acc

    @pl.when(j >= N_P_TILES)
    def _():
        qkv_ref[...] = (acc * cs_ref[...]).astype(BF16)

    @pl.when(j == N_P_TILES + 1)
    def _():
        k_ref[...] = acc

    @pl.when(j == N_P_TILES + 2)
    def _():
        v_ref[...] = acc


def _proj_in(x2d, g, w_bf16, col_scale, layer, depth, k_all, v_all):
    m, d = x2d.shape
    n_tiles = w_bf16.shape[1] // d
    assert n_tiles == N_P_TILES + N_QKV_TILES
    tm = min(1024, m)
    aliased = k_all is not None
    qkv_col = lambda i, j: (i, jnp.clip(j - N_P_TILES, 0, N_QKV_TILES - 1))
    in_specs = [
        pl.BlockSpec((tm, d), lambda i, j: (i, 0)),
        pl.BlockSpec((1, d), lambda i, j: (0, 0)),
        pl.BlockSpec((d, d), lambda i, j: (0, j)),
        pl.BlockSpec((1, d), lambda i, j: (0, jnp.clip(j - N_P_TILES, 0, N_QKV_TILES - 1))),
    ]
    args = [x2d, g, w_bf16, col_scale]
    aliases = {}
    if aliased:
        in_specs += [pl.BlockSpec(memory_space=pl.ANY), pl.BlockSpec(memory_space=pl.ANY)]
        args += [k_all, v_all]
        aliases = {4: 1, 5: 2}
    kv_spec = pl.BlockSpec((None, tm, d), lambda i, j: (layer, i, 0))
    return pl.pallas_call(
        functools.partial(_proj_in_body, aliased=aliased),
        grid=(m // tm, n_tiles),
        in_specs=in_specs,
        out_specs=[
            pl.BlockSpec((tm, d), lambda i, j: (i, jnp.minimum(j, N_P_TILES - 1))),
            kv_spec, kv_spec,
            pl.BlockSpec((tm, d), qkv_col),
        ],
        out_shape=[
            jax.ShapeDtypeStruct((m, N_P_TILES * d), F32),
            jax.ShapeDtypeStruct((depth, m, d), F32),
            jax.ShapeDtypeStruct((depth, m, d), F32),
            jax.ShapeDtypeStruct((m, N_QKV_TILES * d), BF16),
        ],
        scratch_shapes=[pltpu.VMEM((tm, d), BF16)],
        input_output_aliases=aliases,
        compiler_params=_cparams(("arbitrary", "arbitrary")),
        name="proj_in",
    )(*args)


def _lru_body(x_ref, prev_ref, h0_ref, cw_ref, cb_ref, wg_ref, gb_ref, lam_ref,
              ya_ref, convnew_ref, hlast_ref, tail_scr, h_scr, *, tt):
    t = pl.program_id(2)
    nt = pl.num_programs(2)

    @pl.when(t == 0)
    def _():
        tail_scr[...] = jnp.zeros_like(tail_scr)
        tail_scr[SUBLANES - (CONV_A - 1):SUBLANES, :] = prev_ref[0]
        h_scr[...] = h0_ref[0]

    x = x_ref[0]
    tail = tail_scr[...]
    y = cb_ref[...] + cw_ref[CONV_A - 1:CONV_A, :] * x
    for s in range(1, CONV_A):
        y = y + cw_ref[CONV_A - 1 - s:CONV_A - s, :] * _shift_rows(x, tail, s)
    tail_scr[...] = x_ref[0, tt - SUBLANES:tt, :]

    gates = _dot(y.astype(BF16), wg_ref[0]) + gb_ref[0]
    r = _sigmoid(gates[:, :LANES])
    i = _sigmoid(gates[:, LANES:])
    log_a = (-C_LRU) * r * _softplus(-lam_ref[...])
    a = jnp.exp(log_a)
    th = jnp.tanh(log_a)
    b = jnp.sqrt(-2.0 * th / (1.0 - th)) * (i * y)

    row = lax.broadcasted_iota(jnp.int32, (tt, LANES), 0)
    b = b + jnp.where(row == 0, a * h_scr[...], 0.0)
    s = 1
    while s < tt:
        if s < SUBLANES:
            a_sh = jnp.where(row < s, 1.0, pltpu.roll(a, s, 0))
            b_sh = jnp.where(row < s, 0.0, pltpu.roll(b, s, 0))
        else:
            a_sh = jnp.concatenate([jnp.ones((s, LANES), F32), a[:tt - s]], axis=0)
            b_sh = jnp.concatenate([jnp.zeros((s, LANES), F32), b[:tt - s]], axis=0)
        b = a * b_sh + b
        a = a * a_sh
        s *= 2
    h_last = b[tt - 1:tt]
    ya_ref[0] = b.astype(BF16)
    h_scr[...] = h_last

    @pl.when(t == nt - 1)
    def _():
        convnew_ref[0] = x_ref[0, tt - (CONV_A - 1):tt, :]
        hlast_ref[0] = h_last


def _lru(p3d, conv_prev, h0, conv_w, conv_b, wg, gb, lam):
    bsz, t_len, _ = p3d.shape
    d_a = conv_w.shape[1]
    nc = d_a // LANES
    tt = min(512, t_len)
    body = functools.partial(_lru_body, tt=tt)
    return pl.pallas_call(
        body,
        grid=(bsz, nc, t_len // tt),
        in_specs=[
            pl.BlockSpec((1, tt, LANES), lambda b, c, t: (b, t, c)),
            pl.BlockSpec((1, CONV_A - 1, LANES), lambda b, c, t: (b, 0, c)),
            pl.BlockSpec((1, 1, LANES), lambda b, c, t: (b, 0, c)),
            pl.BlockSpec((CONV_A, LANES), lambda b, c, t: (0, c)),
            pl.BlockSpec((1, LANES), lambda b, c, t: (0, c)),
            pl.BlockSpec((1, LANES, 2 * LANES), lambda b, c, t: (c, 0, 0)),
            pl.BlockSpec((1, 1, 2 * LANES), lambda b, c, t: (c, 0, 0)),
            pl.BlockSpec((1, LANES), lambda b, c, t: (0, c)),
        ],
        out_specs=[
            pl.BlockSpec((1, tt, LANES), lambda b, c, t: (b, t, c)),
            pl.BlockSpec((1, CONV_A - 1, LANES), lambda b, c, t: (b, 0, c)),
            pl.BlockSpec((1, 1, LANES), lambda b, c, t: (b, 0, c)),
        ],
        out_shape=[
            jax.ShapeDtypeStruct((bsz, t_len, d_a), BF16),
            jax.ShapeDtypeStruct((bsz, CONV_A - 1, d_a), F32),
            jax.ShapeDtypeStruct((bsz, 1, d_a), F32),
        ],
        scratch_shapes=[pltpu.VMEM((SUBLANES, LANES), F32), pltpu.VMEM((1, LANES), F32)],
        compiler_params=_cparams(("arbitrary", "arbitrary", "arbitrary")),
        name="lru",
    )(p3d, conv_prev, h0, conv_w, conv_b, wg, gb, lam)


def _suffix_sum(l, u):
    hi = l.astype(BF16)
    lo = (l - hi.astype(F32)).astype(BF16)
    return _dot(hi, u) + _dot(lo, u)


def _attn_block(q, kb, vb, u, acc, carry, mask):
    z = _dot_nt(q, kb)
    sp = _softplus(z)
    l = -sp if mask is None else jnp.where(mask, -sp, 0.0)
    between = _suffix_sum(l, u) + carry
    w = jnp.exp(z - sp + between)
    if mask is not None:
        w = jnp.where(mask, w, 0.0)
    return acc + _dot(w.astype(BF16), vb), jnp.sum(l, axis=1, keepdims=True)


def _walk_back(qs, states, n_blocks, load, u):
    def any_alive(jj, sts):
        m = None
        for nb, (_, carry) in zip(n_blocks, sts):
            c = jnp.where(nb - jj > 0, carry, ATT_LOG_DEAD)
            m = c if m is None else jnp.maximum(m, c)
        return (jnp.max(m) > ATT_LOG_CUTOFF).astype(jnp.int32)

    def cond(st):
        return st[1] > 0

    def body(st):
        jj, _, sts = st
        new = []
        for q, nb, (acc, carry) in zip(qs, n_blocks, sts):
            rem = nb - 1 - jj
            valid = rem >= 0
            kb, vb = load(jnp.maximum(rem, 0))
            acc, row_sum = _attn_block(q, kb, vb, u, acc, jnp.where(valid, carry, ATT_LOG_DEAD), None)
            new.append((acc, jnp.where(valid, carry + row_sum, carry)))
        jj = jj + 1
        return jj, any_alive(jj, new), new

    jj0 = jnp.int32(0)
    return lax.while_loop(cond, body, (jj0, any_alive(jj0, states), states))[2]


def _attn_body(*refs, tq, n_streams, tkp, n_past):
    if n_past:
        q_ref, k_ref, v_ref, u_ref, kp_ref, vp_ref, up_ref, o_ref = refs
    else:
        q_ref, k_ref, v_ref, u_ref, o_ref = refs
    i = pl.program_id(2)
    u = u_ref[...]
    row = lax.broadcasted_iota(jnp.int32, (tq, tq), 0)
    col = lax.broadcasted_iota(jnp.int32, (tq, tq), 1)

    def load_new(j):
        start = pl.multiple_of(j * tq, tq)
        return k_ref[0, pl.ds(start, tq), :], v_ref[0, pl.ds(start, tq), :]

    qs, states, n_new = [], [], []
    for s in range(n_streams):
        qi = i * n_streams + s
        q = q_ref[0, s * tq:(s + 1) * tq, :]
        kd, vd = load_new(qi)
        acc, row_sum = _attn_block(q, kd, vd, u, jnp.zeros((tq, LANES), F32), jnp.zeros((tq, 1), F32), col < row)
        qs.append(q)
        states.append((acc, row_sum))
        n_new.append(qi)
    states = _walk_back(qs, states, n_new, load_new, u)

    if n_past:
        def load_past(j):
            start = pl.multiple_of(j * tkp, tkp)
            return kp_ref[0, pl.ds(start, tkp), :].astype(BF16), vp_ref[0, pl.ds(start, tkp), :].astype(BF16)

        states = _walk_back(qs, states, [n_past] * n_streams, load_past, up_ref[...])

    for s in range(n_streams):
        o_ref[0, s * tq:(s + 1) * tq, :] = states[s][0].astype(BF16)


def _strict_upper(n):
    return jnp.asarray(np.triu(np.ones((n, n), np.float32), 1).T, dtype=BF16)


def _attn(qkv3d, k_past, v_past):
    bsz, t_len, d3 = qkv3d.shape
    d_b = d3 // 3
    nh = d_b // LANES
    tq = min(ATT_BLOCK, t_len)
    n_streams = min(ATT_STREAMS, t_len // tq)
    tstep = tq * n_streams
    args = [qkv3d, qkv3d, qkv3d, _strict_upper(tq)]
    in_specs = [
        pl.BlockSpec((1, tstep, LANES), lambda b, h, i: (b, i, h)),
        pl.BlockSpec((1, t_len, LANES), lambda b, h, i: (b, 0, nh + h)),
        pl.BlockSpec((1, t_len, LANES), lambda b, h, i: (b, 0, 2 * nh + h)),
        pl.BlockSpec((tq, tq), lambda b, h, i: (0, 0)),
    ]
    tkp, n_past = 0, 0
    if k_past is not None:
        past = k_past.shape[1]
        tkp = min(ATT_BLOCK, past)
        n_past = past // tkp
        args += [k_past, v_past, _strict_upper(tkp)]
        in_specs += [
            pl.BlockSpec((1, past, LANES), lambda b, h, i: (b, 0, h)),
            pl.BlockSpec((1, past, LANES), lambda b, h, i: (b, 0, h)),
            pl.BlockSpec((tkp, tkp), lambda b, h, i: (0, 0)),
        ]
    body = functools.partial(_attn_body, tq=tq, n_streams=n_streams, tkp=tkp, n_past=n_past)
    return pl.pallas_call(
        body,
        grid=(bsz, nh, t_len // tstep),
        in_specs=in_specs,
        out_specs=pl.BlockSpec((1, tstep, LANES), lambda b, h, i: (b, i, h)),
        out_shape=jax.ShapeDtypeStruct((bsz, t_len, d_b), BF16),
        compiler_params=_cparams(("arbitrary", "arbitrary", "arbitrary")),
        name="attn",
    )(*args)


def _hgrn_tables():
    c = HGRN_CHUNK
    n_seg = 2 + HGRN_LEVELS
    d = np.zeros((n_seg * c, c), np.float32)
    jj = np.arange(c)
    late = np.zeros((HGRN_LEVELS, c, LANES), np.float32)
    early = np.zeros((HGRN_LEVELS, c, LANES), np.float32)
    amask = np.zeros((HGRN_LEVELS + 1, c, c), np.float32)
    amask[HGRN_LEVELS] = np.eye(c)
    for r in range(c):
        d[r] = jj <= r
        d[c + r] = jj > r
    for li in range(HGRN_LEVELS):
        m = c >> (li + 1)
        for r in range(c):
            blk, pos = divmod(r, 2 * m)
            boundary = blk * 2 * m + m - 1
            if pos >= m:
                d[(2 + li) * c + r] = (jj > boundary) & (jj <= r)
                late[li, r] = 1.0
            else:
                d[(2 + li) * c + r] = (jj > r) & (jj <= boundary)
                early[li, r] = 1.0
        blk_id = jj // (2 * m)
        amask[li] = blk_id[:, None] == blk_id[None, :]
    d3 = np.concatenate([d, d, d], axis=1)
    return (jnp.asarray(d3, dtype=BF16), jnp.asarray(late), jnp.asarray(early), jnp.asarray(amask))


def _hgrn_chunk_head(qr, fr, v, gr, st, lb, log_lb, log_1mlb, ng, d3, late_ref, early_ref, amask_ref):
    c = HGRN_CHUNK
    q = qr * _sigmoid(qr)
    ls = jnp.minimum(fr, 0.0) - jnp.log1p(jnp.exp(-jnp.abs(fr)))
    x2 = log_1mlb + ls
    mx = jnp.maximum(log_lb, x2)
    mn = jnp.minimum(log_lb, x2)
    lf = mx + jnp.log1p(jnp.exp(mn - mx))
    kk = (1.0 - lb) * _sigmoid(-fr)

    hi = lf.astype(BF16)
    r1 = lf - hi.astype(F32)
    mid = r1.astype(BF16)
    lo = (r1 - mid.astype(F32)).astype(BF16)
    ex = jnp.exp(_dot(d3, jnp.concatenate([hi, mid, lo], axis=0)))
    xb = ex[0:c]
    xr = ex[c:2 * c]

    o = _dot_nt((q * xb).astype(BF16), st.astype(BF16))
    att = amask_ref[HGRN_LEVELS] * _dot_nt(q.astype(BF16), kk.astype(BF16))
    for li in range(HGRN_LEVELS):
        xm = ex[(2 + li) * c:(3 + li) * c]
        qm = (q * xm * late_ref[li]).astype(BF16)
        km = (kk * xm * early_ref[li]).astype(BF16)
        att = att + amask_ref[li] * _dot_nt(qm, km)
    vb = v.astype(BF16)
    o = o + _dot(att.astype(BF16), vb)
    st_new = st * xb[c - 1:c, :] + _dot_tn(vb, (kk * xr).astype(BF16))
    y = _rms_norm(o, ng) * (gr * _sigmoid(gr))
    return y, st_new


def _hgrn_body(q_ref, f_ref, i_ref, g_ref, s0_ref, gamma_ref, ng_ref, d3_ref, late_ref, early_ref, amask_ref,
               y_ref, slast_ref, st_scr, *, tt, layer):
    t = pl.program_id(1)
    nt = pl.num_programs(1)
    c = HGRN_CHUNK

    @pl.when(t == 0)
    def _():
        for h in range(N_HEADS):
            st_scr[h] = s0_ref[0, h].T

    gam = gamma_ref[...]
    e = jnp.exp(gam - jnp.max(gam, axis=0, keepdims=True))
    lb_all = jnp.sum(e[1:layer + 1], axis=0, keepdims=True) / jnp.sum(e, axis=0, keepdims=True) \
        if layer > 0 else jnp.zeros((1, N_HEADS * LANES), F32)
    log_lb_all = jnp.log(lb_all)
    log_1mlb_all = jnp.log1p(-lb_all)
    ng = ng_ref[...]
    d3 = d3_ref[...]

    def chunk(ci, carry):
        r0 = pl.multiple_of(ci * c, c)
        for h in range(N_HEADS):
            cols = slice(h * LANES, (h + 1) * LANES)
            y, st_new = _hgrn_chunk_head(
                q_ref[0, pl.ds(r0, c), cols], f_ref[0, pl.ds(r0, c), cols],
                i_ref[0, pl.ds(r0, c), cols], g_ref[0, pl.ds(r0, c), cols],
                st_scr[h], lb_all[:, cols], log_lb_all[:, cols], log_1mlb_all[:, cols],
                ng, d3, late_ref, early_ref, amask_ref)
            st_scr[h] = st_new
            y_ref[0, pl.ds(r0, c), cols] = y.astype(BF16)
        return carry

    lax.fori_loop(0, tt // c, chunk, 0)

    @pl.when(t == nt - 1)
    def _():
        for h in range(N_HEADS):
            slast_ref[0, h] = st_scr[h].T


def _hgrn(p3d, s0, gamma, norm_g, layer, tile0):
    bsz, t_len, _ = p3d.shape
    nh = N_HEADS
    d = nh * LANES
    depth = gamma.shape[0]
    tt = min(512, t_len)
    d3, late, early, amask = _hgrn_tables()
    body = functools.partial(_hgrn_body, tt=tt, layer=layer)

    def col_spec(off):
        return pl.BlockSpec((1, tt, d), lambda b, t: (b, t, tile0 + off))

    def const_spec(shape):
        return pl.BlockSpec(shape, lambda b, t: (0,) * len(shape))

    return pl.pallas_call(
        body,
        grid=(bsz, t_len // tt),
        in_specs=[
            col_spec(0), col_spec(1), col_spec(2), col_spec(3),
            pl.BlockSpec((1, nh, LANES, LANES), lambda b, t: (b, 0, 0, 0)),
            const_spec((depth, d)),
            const_spec((1, LANES)),
            const_spec(d3.shape), const_spec(late.shape), const_spec(early.shape), const_spec(amask.shape),
        ],
        out_specs=[
            pl.BlockSpec((1, tt, d), lambda b, t: (b, t, 0)),
            pl.BlockSpec((1, nh, LANES, LANES), lambda b, t: (b, 0, 0, 0)),
        ],
        out_shape=[
            jax.ShapeDtypeStruct((bsz, t_len, d), BF16),
            jax.ShapeDtypeStruct((bsz, nh, LANES, LANES), F32),
        ],
        scratch_shapes=[pltpu.VMEM((nh, LANES, LANES), F32)],
        compiler_params=_cparams(("arbitrary", "arbitrary")),
        name="hgrn",
    )(p3d, p3d, p3d, p3d, s0, gamma, norm_g, d3, late, early, amask)


def _merge_body(x_ref, ya_ref, yb_ref, yc_ref, ga_ref, gb_ref, gc_ref, wb_ref, wo_ref, o_ref):
    merged = _sigmoid(ga_ref[...]) * _dot(ya_ref[...], wb_ref[0])
    merged = merged + _sigmoid(gb_ref[...]) * _dot(yb_ref[...], wb_ref[1])
    merged = merged + _sigmoid(gc_ref[...]) * _dot(yc_ref[...], wb_ref[2])
    o_ref[...] = x_ref[...] + _dot(merged.astype(BF16), wo_ref[...])


def _merge(x2d, ya, yb, yc, p2d, w_branch, w_out, gtile0):
    m, d = x2d.shape
    tm = min(512, m)
    row = lambda i: (i, 0)
    return pl.pallas_call(
        _merge_body,
        grid=(m // tm,),
        in_specs=[
            pl.BlockSpec((tm, d), row),
            pl.BlockSpec((tm, d), row), pl.BlockSpec((tm, d), row), pl.BlockSpec((tm, d), row),
            pl.BlockSpec((tm, d), lambda i: (i, gtile0)),
            pl.BlockSpec((tm, d), lambda i: (i, gtile0 + 1)),
            pl.BlockSpec((tm, d), lambda i: (i, gtile0 + 2)),
            pl.BlockSpec((N_BRANCH, d, d), lambda i: (0, 0, 0), pipeline_mode=pl.Buffered(1)),
            pl.BlockSpec((d, d), lambda i: (0, 0), pipeline_mode=pl.Buffered(1)),
        ],
        out_specs=pl.BlockSpec((tm, d), row),
        out_shape=jax.ShapeDtypeStruct((m, d), F32),
        compiler_params=_cparams(("arbitrary",)),
        name="merge",
    )(x2d, ya, yb, yc, p2d, p2d, p2d, w_branch, w_out)


def _gelu_tanh(x):
    return 0.5 * x * (1.0 + jnp.tanh(np.sqrt(2.0 / np.pi).astype(np.float32) * (x + 0.044715 * (x * x * x))))


def _ffn_body(*refs, tm, final):
    if final:
        (x_ref, g_ref, wu_ref, wv_ref, wd_ref, cw_ref, cb_ref, prev_ref, gf_ref,
         o_ref, convnew_ref, y_ref, h_scr, acc_scr, tail_scr) = refs
    else:
        (x_ref, g_ref, wu_ref, wv_ref, wd_ref, cw_ref, cb_ref, prev_ref,
         o_ref, convnew_ref, h_scr, acc_scr, tail_scr) = refs
    t = pl.program_id(1)
    nt = pl.num_programs(1)
    k = pl.program_id(2)
    nk = pl.num_programs(2)

    @pl.when(k == 0)
    def _():
        h_scr[...] = _rms_norm(x_ref[0], g_ref[...]).astype(BF16)
        acc_scr[...] = jnp.zeros_like(acc_scr)

    @pl.when(t == 0)
    def _():
        tail_scr[k] = jnp.zeros(tail_scr.shape[1:], F32)
        tail_scr[k, SUBLANES - (CONV_F - 1):SUBLANES, :] = prev_ref[0]

    h = h_scr[...]
    u = _dot(h, wu_ref[...])
    v = _dot(h, wv_ref[...])
    tail = tail_scr[k]
    uc = cb_ref[...] + cw_ref[CONV_F - 1:CONV_F, :] * u
    for s in range(1, CONV_F):
        uc = uc + cw_ref[CONV_F - 1 - s:CONV_F - s, :] * _shift_rows(u, tail, s)
    tail_scr[k] = u[tm - SUBLANES:tm]
    acc_scr[...] += _dot((_gelu_tanh(uc) * v).astype(BF16), wd_ref[...])

    @pl.when(t == nt - 1)
    def _():
        convnew_ref[0] = u[tm - (CONV_F - 1):tm]

    @pl.when(k == nk - 1)
    def _():
        x_new = x_ref[0] + acc_scr[...]
        o_ref[0] = x_new
        if final:
            y_ref[0] = _rms_norm(x_new, gf_ref[...])


def _ffn(x3d, g, w_up, w_down, conv_w, conv_b, conv_prev, final_g):
    bsz, t_len, d = x3d.shape
    d_ff = w_down.shape[0]
    tm = min(512, t_len)
    nt = t_len // tm
    kc = 512
    nk = d_ff // kc
    final = final_g is not None
    body = functools.partial(_ffn_body, tm=tm, final=final)
    xspec = pl.BlockSpec((1, tm, d), lambda b, t, k: (b, t, 0))
    in_specs = [
        xspec,
        pl.BlockSpec((1, d), lambda b, t, k: (0, 0)),
        pl.BlockSpec((d, kc), lambda b, t, k: (0, k)),
        pl.BlockSpec((d, kc), lambda b, t, k: (0, nk + k)),
        pl.BlockSpec((kc, d), lambda b, t, k: (k, 0)),
        pl.BlockSpec((CONV_F, kc), lambda b, t, k: (0, k)),
        pl.BlockSpec((1, kc), lambda b, t, k: (0, k)),
        pl.BlockSpec((1, CONV_F - 1, kc), lambda b, t, k: (b, 0, k)),
    ]
    args = [x3d, g, w_up, w_up, w_down, conv_w, conv_b, conv_prev]
    convnew_spec = pl.BlockSpec((1, CONV_F - 1, kc), lambda b, t, k: (b, 0, jnp.where(t == nt - 1, k, 0)))
    out_specs = [xspec, convnew_spec]
    out_shape = [jax.ShapeDtypeStruct((bsz, t_len, d), F32), jax.ShapeDtypeStruct((bsz, CONV_F - 1, d_ff), F32)]
    if final:
        in_specs.append(pl.BlockSpec((1, d), lambda b, t, k: (0, 0)))
        args.append(final_g)
        out_specs.append(xspec)
        out_shape.append(jax.ShapeDtypeStruct((bsz, t_len, d), F32))
    return pl.pallas_call(
        body,
        grid=(bsz, nt, nk),
        in_specs=in_specs,
        out_specs=out_specs,
        out_shape=out_shape,
        scratch_shapes=[pltpu.VMEM((tm, d), BF16), pltpu.VMEM((tm, d), F32), pltpu.VMEM((nk, SUBLANES, kc), F32)],
        compiler_params=_cparams(("arbitrary", "arbitrary", "arbitrary")),
        name="ffn",
    )(*args)


def _lru_gate_weights(gate_w, gate_b):
    n_groups = gate_w.shape[1]
    per = LANES // LRU_GROUP
    nc = n_groups // per
    w = gate_w.reshape(2, nc, per, LRU_GROUP, LRU_GROUP)
    eye = jnp.eye(per, dtype=gate_w.dtype)
    wg = jnp.einsum('gcpij,pq->cpigqj', w, eye).reshape(nc, LANES, 2 * LANES)
    gb = gate_b.reshape(2, nc, LANES).transpose(1, 0, 2).reshape(nc, 1, 2 * LANES)
    return wg.astype(BF16), gb


def _regroup_w_in(w, d):
    tiles = w.reshape(d, -1, d)
    order = np.array([0, 4, 5, 6, 7, 8, 9, 10, 1, 2, 3])
    return tiles[:, order, :].reshape(d, -1).astype(BF16)


def _layer(x3d, conv_a_prev, h0, k_past, v_past, s0, conv_f_prev, layer, depth, k_all, v_all, final_g,
           norm_mix_g, w_in, conv_a_w, conv_a_b, lru_gate_w, lru_gate_b, lru_lambda, hgrn_gamma,
           hgrn_norm_g, w_branch, w_out, norm_ffn_g, w_up, conv_f_w, conv_f_b, w_down):
    bsz, t_len, d = x3d.shape
    m = bsz * t_len
    x2d = x3d.reshape(m, d)
    col_scale = jnp.concatenate([jnp.full((1, d), LANES ** -0.5, F32), jnp.ones((1, 2 * d), F32)], axis=1)
    p2d, k_all, v_all, qkv = _proj_in(x2d, norm_mix_g[layer][None], _regroup_w_in(w_in[layer], d), col_scale,
                                      layer, depth, k_all, v_all)
    p3d = p2d.reshape(bsz, t_len, -1)

    wg, gb = _lru_gate_weights(lru_gate_w[layer], lru_gate_b[layer])
    ya, conv_a_new, h_last = _lru(p3d, conv_a_prev, h0[:, None, :], conv_a_w[layer], conv_a_b[layer][None],
                                  wg, gb, lru_lambda[layer][None])

    if k_past is not None:
        k_past = k_past.reshape(bsz, k_past.shape[1], d)
        v_past = v_past.reshape(bsz, v_past.shape[1], d)
    yb = _attn(qkv.reshape(bsz, t_len, N_QKV_TILES * d), k_past, v_past)

    yc, s_last = _hgrn(p3d, s0, hgrn_gamma, hgrn_norm_g[layer][None], layer, 1)

    x1 = _merge(x2d, ya.reshape(m, d), yb.reshape(m, d), yc.reshape(m, d), p2d,
                w_branch[layer].astype(BF16), w_out[layer].astype(BF16), N_P_TILES - N_BRANCH)

    outs = _ffn(x1.reshape(bsz, t_len, d), norm_ffn_g[layer][None], w_up[layer].astype(BF16),
                w_down[layer].astype(BF16), conv_f_w[layer], conv_f_b[layer][None], conv_f_prev, final_g)
    x_new, conv_f_new = outs[0], outs[1]
    y = outs[2] if final_g is not None else None
    return x_new, y, k_all, v_all, (conv_a_new, h_last[:, 0, :], s_last, conv_f_new)


def _run_group(x, conv_a, lru, k_past, v_past, hgrn, conv_f, params, final_g, depth):
    bsz, t_len, d = x.shape
    k_all = v_all = y = None
    states = []
    for l in range(depth):
        x, y, k_all, v_all, st = _layer(
            x, conv_a[l], lru[l], None if k_past is None else k_past[l], None if v_past is None else v_past[l],
            hgrn[l], conv_f[l], l, depth, k_all, v_all, final_g if l == depth - 1 else None, *params)
        states.append(st)
    conv_a_new, lru_new, hgrn_new, conv_f_new = (jnp.stack(z) for z in zip(*states))
    kv_shape = (depth, bsz, t_len, N_HEADS, LANES)
    return y, conv_a_new, lru_new, k_all.reshape(kv_shape), v_all.reshape(kv_shape), hgrn_new, conv_f_new


def kernel(x_prompt, x_sample, cache_conv_a, state_lru, cache_k, cache_v, state_hgrn, cache_conv_ffn, norm_mix_g, w_in, conv_a_w, conv_a_b, lru_gate_w, lru_gate_b, lru_lambda, hgrn_gamma, hgrn_norm_g, w_branch, w_out, norm_ffn_g, w_up, conv_f_w, conv_f_b, w_down, norm_final_g):
    depth = w_in.shape[0]
    d = x_prompt.shape[-1]
    d_ff = w_down.shape[1]
    assert d == N_HEADS * LANES and d_ff == 3 * d and w_in.shape[2] == (N_P_TILES + N_QKV_TILES) * d
    params = (norm_mix_g, w_in, conv_a_w, conv_a_b, lru_gate_w, lru_gate_b, lru_lambda, hgrn_gamma,
              hgrn_norm_g, w_branch, w_out, norm_ffn_g, w_up, conv_f_w, conv_f_b, w_down)
    final_g = norm_final_g[None]

    bp = x_prompt.shape[0]
    zeros = lambda *shape: [jnp.zeros(shape, F32)] * depth
    (y_prompt, p_conv_a, p_lru, p_k, p_v, p_hgrn, p_conv_ffn) = _run_group(
        x_prompt, zeros(bp, CONV_A - 1, d), zeros(bp, d), None, None,
        zeros(bp, N_HEADS, LANES, LANES), zeros(bp, CONV_F - 1, d_ff), params, final_g, depth)
    (y_sample, s_conv_a, s_lru, s_k, s_v, s_hgrn, s_conv_ffn) = _run_group(
        x_sample, cache_conv_a, state_lru, cache_k, cache_v, state_hgrn, cache_conv_ffn, params, final_g, depth)
    return (y_prompt, y_sample, p_conv_a, s_conv_a, p_lru, s_lru, p_k, s_k, p_v, s_v,
            p_hgrn, s_hgrn, p_conv_ffn, s_conv_ffn)
```

```python
import functools

import numpy as np
import jax
import jax.numpy as jnp
from jax import lax
from jax.experimental import pallas as pl
from jax.experimental.pallas import tpu as pltpu

F32 = jnp.float32
BF16 = jnp.bfloat16

EPS = 1e-6
LANES = 128
SUBLANES = 8
VMEM_LIMIT = 48 * 1024 * 1024

N_HEADS = 8
LRU_GROUP = 64
C_LRU = 8.0
CONV_A = 4
CONV_F = 3
N_BRANCH = 3
HGRN_CHUNK = 64
HGRN_LEVELS = 6
HGRN_HEAD_GROUP = 2
ATT_BLOCK = 256
ATT_STREAMS = 8
ATT_LOG_CUTOFF = -104.0
ATT_LOG_DEAD = -1e30


def _cparams(semantics):
    return pltpu.CompilerParams(dimension_semantics=semantics, vmem_limit_bytes=VMEM_LIMIT)


def _sigmoid(x):
    return 1.0 / (1.0 + jnp.exp(-x))


def _softplus(x):
    return jnp.maximum(x, 0.0) + jnp.log1p(jnp.exp(-jnp.abs(x)))


def _rms_norm(x, g):
    ms = jnp.mean(x * x, axis=-1, keepdims=True)
    return x * lax.rsqrt(ms + EPS) * g


def _dot(a, b):
    return jnp.dot(a, b, preferred_element_type=F32)


def _dot_nt(a, b):
    return lax.dot_general(a, b, (((1,), (1,)), ((), ())), preferred_element_type=F32)


def _dot_tn(a, b):
    return lax.dot_general(a, b, (((0,), (0,)), ((), ())), preferred_element_type=F32)


def _shift_rows(x, tail, s):
    n_rows = x.shape[0]
    rows = lax.broadcasted_iota(jnp.int32, (SUBLANES, x.shape[1]), 0)
    rolled = pltpu.roll(x, s, 0)
    first = jnp.where(rows < s, pltpu.roll(tail, s, 0), rolled[0:SUBLANES])
    if n_rows == SUBLANES:
        return first
    return jnp.concatenate([first, rolled[SUBLANES:]], axis=0)


N_P_TILES = 8
N_QKV_TILES = 3


def _proj_in_body(*refs, aliased):
    if aliased:
        x_ref, g_ref, w_ref, cs_ref, _, _, p_ref, k_ref, v_ref, qkv_ref, h_scr = refs
    else:
        x_ref, g_ref, w_ref, cs_ref, p_ref, k_ref, v_ref, qkv_ref, h_scr = refs
    j = pl.program_id(1)

    @pl.when(j == 0)
    def _():
        h_scr[...] = _rms_norm(x_ref[...], g_ref[...]).astype(BF16)

    @pl.when(j < N_P_TILES)
    def _():
        p_ref[...] = _dot(h_scr[...], w_ref[...])

    @pl.when(j == N_P_TILES)
    def _():
        qkv_ref[...] = (_dot(h_scr[...], w_ref[...]) * cs_ref[...]).astype(BF16)

    for off, dst_ref in ((1, k_ref), (2, v_ref)):
        @pl.when(j == N_P_TILES + off)
        def _(dst_ref=dst_ref):
            acc = _dot(h_scr[...], w_ref[...])
            dst_ref[...] = acc
            qkv_ref[...] = (acc * cs_ref[...]).astype(BF16)


def _proj_in(x2d, g, w_bf16, col_scale, layer, depth, k_all, v_all):
    m, d = x2d.shape
    n_tiles = w_bf16.shape[1] // d
    assert n_tiles == N_P_TILES + N_QKV_TILES
    tm = min(1024, m)
    aliased = k_all is not None
    qkv_col = lambda i, j: (i, jnp.clip(j - N_P_TILES, 0, N_QKV_TILES - 1))
    in_specs = [
        pl.BlockSpec((tm, d), lambda i, j: (i, 0)),
        pl.BlockSpec((1, d), lambda i, j: (0, 0)),
        pl.BlockSpec((d, d), lambda i, j: (0, j)),
        pl.BlockSpec((1, d), lambda i, j: (0, jnp.clip(j - N_P_TILES, 0, N_QKV_TILES - 1))),
    ]
    args = [x2d, g, w_bf16, col_scale]
    aliases = {}
    if aliased:
        in_specs += [pl.BlockSpec(memory_space=pl.ANY), pl.BlockSpec(memory_space=pl.ANY)]
        args += [k_all, v_all]
        aliases = {4: 1, 5: 2}
    kv_spec = pl.BlockSpec((None, tm, d), lambda i, j: (layer, i, 0))
    return pl.pallas_call(
        functools.partial(_proj_in_body, aliased=aliased),
        grid=(m // tm, n_tiles),
        in_specs=in_specs,
        out_specs=[
            pl.BlockSpec((tm, d), lambda i, j: (i, jnp.minimum(j, N_P_TILES - 1))),
            kv_spec, kv_spec,
            pl.BlockSpec((tm, d), qkv_col),
        ],
        out_shape=[
            jax.ShapeDtypeStruct((m, N_P_TILES * d), F32),
            jax.ShapeDtypeStruct((depth, m, d), F32),
            jax.ShapeDtypeStruct((depth, m, d), F32),
            jax.ShapeDtypeStruct((m, N_QKV_TILES * d), BF16),
        ],
        scratch_shapes=[pltpu.VMEM((tm, d), BF16)],
        input_output_aliases=aliases,
        compiler_params=_cparams(("arbitrary", "arbitrary")),
        name="proj_in",
    )(*args)


def _lru_body(x_ref, prev_ref, h0_ref, cw_ref, cb_ref, wg_ref, gb_ref, lam_ref,
              ya_ref, convnew_ref, hlast_ref, tail_scr, h_scr, *, tt):
    t = pl.program_id(2)
    nt = pl.num_programs(2)

    @pl.when(t == 0)
    def _():
        tail_scr[...] = jnp.zeros_like(tail_scr)
        tail_scr[SUBLANES - (CONV_A - 1):SUBLANES, :] = prev_ref[0]
        h_scr[...] = h0_ref[0]

    x = x_ref[0]
    tail = tail_scr[...]
    y = cb_ref[...] + cw_ref[CONV_A - 1:CONV_A, :] * x
    for s in range(1, CONV_A):
        y = y + cw_ref[CONV_A - 1 - s:CONV_A - s, :] * _shift_rows(x, tail, s)
    tail_scr[...] = x_ref[0, tt - SUBLANES:tt, :]

    gates = _dot(y.astype(BF16), wg_ref[0]) + gb_ref[0]
    r = _sigmoid(gates[:, :LANES])
    i = _sigmoid(gates[:, LANES:])
    log_a = (-C_LRU) * r * _softplus(-lam_ref[...])
    a = jnp.exp(log_a)
    th = jnp.tanh(log_a)
    b = jnp.sqrt(-2.0 * th / (1.0 - th)) * (i * y)

    row = lax.broadcasted_iota(jnp.int32, (tt, LANES), 0)
    b = b + jnp.where(row == 0, a * h_scr[...], 0.0)
    s = 1
    while s < tt:
        if s < SUBLANES:
            a_sh = jnp.where(row < s, 1.0, pltpu.roll(a, s, 0))
            b_sh = jnp.where(row < s, 0.0, pltpu.roll(b, s, 0))
        else:
            a_sh = jnp.concatenate([jnp.ones((s, LANES), F32), a[:tt - s]], axis=0)
            b_sh = jnp.concatenate([jnp.zeros((s, LANES), F32), b[:tt - s]], axis=0)
        b = a * b_sh + b
        a = a * a_sh
        s *= 2
    h_last = b[tt - 1:tt]
    ya_ref[0] = b.astype(BF16)
    h_scr[...] = h_last

    @pl.when(t == nt - 1)
    def _():
        convnew_ref[0] = x_ref[0, tt - (CONV_A - 1):tt, :]
        hlast_ref[0] = h_last


def _lru(p3d, conv_prev, h0, conv_w, conv_b, wg, gb, lam):
    bsz, t_len, _ = p3d.shape
    d_a = conv_w.shape[1]
    nc = d_a // LANES
    tt = min(512, t_len)
    body = functools.partial(_lru_body, tt=tt)
    return pl.pallas_call(
        body,
        grid=(bsz, nc, t_len // tt),
        in_specs=[
            pl.BlockSpec((1, tt, LANES), lambda b, c, t: (b, t, c)),
            pl.BlockSpec((1, CONV_A - 1, LANES), lambda b, c, t: (b, 0, c)),
            pl.BlockSpec((1, 1, LANES), lambda b, c, t: (b, 0, c)),
            pl.BlockSpec((CONV_A, LANES), lambda b, c, t: (0, c)),
            pl.BlockSpec((1, LANES), lambda b, c, t: (0, c)),
            pl.BlockSpec((1, LANES, 2 * LANES), lambda b, c, t: (c, 0, 0)),
            pl.BlockSpec((1, 1, 2 * LANES), lambda b, c, t: (c, 0, 0)),
            pl.BlockSpec((1, LANES), lambda b, c, t: (0, c)),
        ],
        out_specs=[
            pl.BlockSpec((1, tt, LANES), lambda b, c, t: (b, t, c)),
            pl.BlockSpec((1, CONV_A - 1, LANES), lambda b, c, t: (b, 0, c)),
            pl.BlockSpec((1, 1, LANES), lambda b, c, t: (b, 0, c)),
        ],
        out_shape=[
            jax.ShapeDtypeStruct((bsz, t_len, d_a), BF16),
            jax.ShapeDtypeStruct((bsz, CONV_A - 1, d_a), F32),
            jax.ShapeDtypeStruct((bsz, 1, d_a), F32),
        ],
        scratch_shapes=[pltpu.VMEM((SUBLANES, LANES), F32), pltpu.VMEM((1, LANES), F32)],
        compiler_params=_cparams(("arbitrary", "arbitrary", "arbitrary")),
        name="lru",
    )(p3d, conv_prev, h0, conv_w, conv_b, wg, gb, lam)


def _suffix_sum(l, u2):
    hi = l.astype(BF16)
    lo = (l - hi.astype(F32)).astype(BF16)
    return _dot(jnp.concatenate([hi, lo], axis=1), u2)


def _attn_block(q, kb, vb, u2, acc, carry, mask):
    z = _dot_nt(q, kb)
    sp = jnp.maximum(z, 0.0) + jnp.log(1.0 + jnp.exp(-jnp.abs(z)))
    l = -sp if mask is None else jnp.where(mask, -sp, 0.0)
    between = _suffix_sum(l, u2) + carry
    w = jnp.exp(z - sp + between)
    if mask is not None:
        w = jnp.where(mask, w, 0.0)
    return acc + _dot(w.astype(BF16), vb), jnp.sum(l, axis=1, keepdims=True)


def _max_log_mass(states, n_left):
    m = None
    for nl, (_, carry) in zip(n_left, states):
        c = jnp.where(nl > 0, carry, ATT_LOG_DEAD)
        m = c if m is None else jnp.maximum(m, c)
    return jnp.max(m)


def _walk_back(qs, states, n_blocks, loads, u2):
    def alive(jj, sts):
        return (_max_log_mass(sts, [nb - jj for nb in n_blocks]) > ATT_LOG_CUTOFF).astype(jnp.int32)

    def cond(st):
        return st[1] > 0

    def body(st):
        jj, _, sts = st
        new = []
        for q, nb, load, (acc, carry) in zip(qs, n_blocks, loads, sts):
            rem = nb - 1 - jj
            valid = rem >= 0
            kb, vb = load(jnp.maximum(rem, 0))
            acc, row_sum = _attn_block(q, kb, vb, u2, acc, jnp.where(valid, carry, ATT_LOG_DEAD), None)
            new.append((acc, jnp.where(valid, carry + row_sum, carry)))
        jj = jj + 1
        return jj, alive(jj, new), new

    jj0 = jnp.int32(0)
    return lax.while_loop(cond, body, (jj0, alive(jj0, states), states))[2]


def _diag_block(q, kd, vd, u2):
    tq = q.shape[0]
    row = lax.broadcasted_iota(jnp.int32, (tq, tq), 0)
    col = lax.broadcasted_iota(jnp.int32, (tq, tq), 1)
    return _attn_block(q, kd, vd, u2, jnp.zeros((tq, LANES), F32), jnp.zeros((tq, 1), F32), col < row)


def _attn_fresh_body(q_ref, k_ref, v_ref, u_ref, o_ref, *, tq, n_streams):
    i = pl.program_id(2)
    u2 = u_ref[...]

    def load(j):
        start = pl.multiple_of(j * tq, tq)
        return k_ref[0, pl.ds(start, tq), :], v_ref[0, pl.ds(start, tq), :]

    qs, states, n_prev = [], [], []
    for s in range(n_streams):
        qi = i * n_streams + s
        q = q_ref[0, s * tq:(s + 1) * tq, :]
        qs.append(q)
        states.append(_diag_block(q, *load(qi), u2))
        n_prev.append(qi)
    states = _walk_back(qs, states, n_prev, [load] * n_streams, u2)
    for s in range(n_streams):
        o_ref[0, s * tq:(s + 1) * tq, :] = states[s][0].astype(BF16)


def _strict_upper2(n):
    u = np.triu(np.ones((n, n), np.float32), 1).T
    return jnp.asarray(np.concatenate([u, u], axis=0), dtype=BF16)


def _attn_fresh(qkv3d):
    bsz, t_len, d3 = qkv3d.shape
    d_b = d3 // 3
    nh = d_b // LANES
    tq = min(ATT_BLOCK, t_len)
    n_streams = min(ATT_STREAMS, t_len // tq)
    tstep = tq * n_streams
    body = functools.partial(_attn_fresh_body, tq=tq, n_streams=n_streams)
    return pl.pallas_call(
        body,
        grid=(bsz, nh, t_len // tstep),
        in_specs=[
            pl.BlockSpec((1, tstep, LANES), lambda b, h, i: (b, i, h)),
            pl.BlockSpec((1, t_len, LANES), lambda b, h, i: (b, 0, nh + h)),
            pl.BlockSpec((1, t_len, LANES), lambda b, h, i: (b, 0, 2 * nh + h)),
            pl.BlockSpec((2 * tq, tq), lambda b, h, i: (0, 0)),
        ],
        out_specs=pl.BlockSpec((1, tstep, LANES), lambda b, h, i: (b, i, h)),
        out_shape=jax.ShapeDtypeStruct((bsz, t_len, d_b), BF16),
        compiler_params=_cparams(("arbitrary", "arbitrary", "arbitrary")),
        name="attn_fresh",
    )(qkv3d, qkv3d, qkv3d, _strict_upper2(tq))


def _attn_cached_body(q_ref, k_ref, v_ref, u_ref, kp_ref, vp_ref, up_ref, o_ref, acc_scr, carry_scr, alive_scr,
                      *, tkp, n_sub):
    p = pl.program_id(1)
    n_chunks = pl.num_programs(1)
    heads = range(N_HEADS)
    qs = [q_ref[0, :, h * LANES:(h + 1) * LANES] for h in heads]

    @pl.when(p == 0)
    def _():
        u2 = u_ref[...]
        for h in heads:
            cols = slice(h * LANES, (h + 1) * LANES)
            acc, carry = _diag_block(qs[h], k_ref[0, :, cols], v_ref[0, :, cols], u2)
            acc_scr[h] = acc
            carry_scr[h] = carry
        alive_scr[0] = 1

    @pl.when(alive_scr[0] > 0)
    def _():
        def load_head(h):
            def load(j):
                rows = pl.ds(j * (tkp * N_HEADS) + h, tkp, stride=N_HEADS)
                return kp_ref[0, rows, :].astype(BF16), vp_ref[0, rows, :].astype(BF16)
            return load

        states = [(acc_scr[h], carry_scr[h]) for h in heads]
        states = _walk_back(qs, states, [n_sub] * N_HEADS, [load_head(h) for h in heads], up_ref[...])
        for h in heads:
            acc_scr[h] = states[h][0]
            carry_scr[h] = states[h][1]
        alive_scr[0] = (_max_log_mass(states, [1] * N_HEADS) > ATT_LOG_CUTOFF).astype(jnp.int32)

    @pl.when(p == n_chunks - 1)
    def _():
        for h in heads:
            o_ref[0, :, h * LANES:(h + 1) * LANES] = acc_scr[h].astype(BF16)


def _attn_cached(qkv3d, k_cache, v_cache, layer):
    bsz, t_len, d3 = qkv3d.shape
    d = d3 // 3
    depth, _, past, nh, dh = k_cache.shape
    assert nh == N_HEADS and dh == LANES and d == nh * dh and t_len <= ATT_BLOCK
    tkp = min(ATT_BLOCK, past)
    n_sub = max(1, 512 // tkp)
    tpc = n_sub * tkp
    n_chunks = past // tpc
    kc = k_cache.reshape(depth, bsz, past * nh, dh)
    vc = v_cache.reshape(depth, bsz, past * nh, dh)
    new_spec = lambda tile: pl.BlockSpec((1, t_len, d), lambda b, p: (b, 0, tile))
    past_spec = pl.BlockSpec((None, 1, tpc * nh, dh), lambda b, p: (layer, b, n_chunks - 1 - p, 0))
    body = functools.partial(_attn_cached_body, tkp=tkp, n_sub=n_sub)
    return pl.pallas_call(
        body,
        grid=(bsz, n_chunks),
        in_specs=[
            new_spec(0), new_spec(1), new_spec(2),
            pl.BlockSpec((2 * t_len, t_len), lambda b, p: (0, 0)),
            past_spec, past_spec,
            pl.BlockSpec((2 * tkp, tkp), lambda b, p: (0, 0)),
        ],
        out_specs=pl.BlockSpec((1, t_len, d), lambda b, p: (b, 0, 0)),
        out_shape=jax.ShapeDtypeStruct((bsz, t_len, d), BF16),
        scratch_shapes=[pltpu.VMEM((nh, t_len, LANES), F32), pltpu.VMEM((nh, t_len, 1), F32),
                        pltpu.SMEM((1,), jnp.int32)],
        compiler_params=_cparams(("arbitrary", "arbitrary")),
        name="attn_cached",
    )(qkv3d, qkv3d, qkv3d, _strict_upper2(t_len), kc, vc, _strict_upper2(tkp))


def _hgrn_tables():
    c = HGRN_CHUNK
    n_seg = 2 + HGRN_LEVELS
    d = np.zeros((n_seg * c, c), np.float32)
    jj = np.arange(c)
    amask = np.zeros((HGRN_LEVELS + 1, c, c), np.float32)
    amask[HGRN_LEVELS] = np.eye(c)
    for r in range(c):
        d[r] = jj <= r
        d[c + r] = jj > r
    for li in range(HGRN_LEVELS):
        m = c >> (li + 1)
        for r in range(c):
            blk, pos = divmod(r, 2 * m)
            boundary = blk * 2 * m + m - 1
            if pos >= m:
                d[(2 + li) * c + r] = (jj > boundary) & (jj <= r)
            else:
                d[(2 + li) * c + r] = (jj > r) & (jj <= boundary)
        late = (jj % (2 * m)) >= m
        blk_id = jj // (2 * m)
        amask[li] = (blk_id[:, None] == blk_id[None, :]) & late[:, None] & ~late[None, :]
    d3 = np.concatenate([d, d, d], axis=1)
    return jnp.asarray(d3, dtype=BF16), jnp.asarray(amask)


def _hgrn_chunk_heads(qr, fr, v, gr, sts, lb, log_lb, log_1mlb, ng, d3, amask_ref):
    c = HGRN_CHUNK
    q = qr * _sigmoid(qr)
    ls = jnp.minimum(fr, 0.0) - jnp.log1p(jnp.exp(-jnp.abs(fr)))
    x2 = log_1mlb + ls
    mx = jnp.maximum(log_lb, x2)
    mn = jnp.minimum(log_lb, x2)
    lf = mx + jnp.log1p(jnp.exp(mn - mx))
    kk = (1.0 - lb) * _sigmoid(-fr)

    hi = lf.astype(BF16)
    r1 = lf - hi.astype(F32)
    mid = r1.astype(BF16)
    lo = (r1 - mid.astype(F32)).astype(BF16)
    ex = jnp.exp(_dot(d3, jnp.concatenate([hi, mid, lo], axis=0)))
    xb = ex[0:c]
    qin = (q * xb).astype(BF16)
    kr = (kk * ex[c:2 * c]).astype(BF16)
    qb, kb, vb = q.astype(BF16), kk.astype(BF16), v.astype(BF16)
    qms = [(q * ex[(2 + li) * c:(3 + li) * c]).astype(BF16) for li in range(HGRN_LEVELS)]
    kms = [(kk * ex[(2 + li) * c:(3 + li) * c]).astype(BF16) for li in range(HGRN_LEVELS)]
    gate = gr * _sigmoid(gr)

    outs = []
    for h, st in enumerate(sts):
        cols = slice(h * LANES, (h + 1) * LANES)
        o = _dot_nt(qin[:, cols], st.astype(BF16))
        att = amask_ref[HGRN_LEVELS] * _dot_nt(qb[:, cols], kb[:, cols])
        for li in range(HGRN_LEVELS):
            att = att + amask_ref[li] * _dot_nt(qms[li][:, cols], kms[li][:, cols])
        o = o + _dot(att.astype(BF16), vb[:, cols])
        st_new = st * xb[c - 1:c, cols] + _dot_tn(vb[:, cols], kr[:, cols])
        outs.append((_rms_norm(o, ng) * gate[:, cols], st_new))
    return outs


def _hgrn_body(q_ref, f_ref, i_ref, g_ref, s0_ref, gamma_ref, ng_ref, d3_ref, amask_ref,
               y_ref, slast_ref, st_scr, *, tt, layer):
    t = pl.program_id(1)
    nt = pl.num_programs(1)
    c = HGRN_CHUNK

    @pl.when(t == 0)
    def _():
        for h in range(N_HEADS):
            st_scr[h] = s0_ref[0, h].T

    gam = gamma_ref[...]
    e = jnp.exp(gam - jnp.max(gam, axis=0, keepdims=True))
    lb_all = jnp.sum(e[1:layer + 1], axis=0, keepdims=True) / jnp.sum(e, axis=0, keepdims=True) \
        if layer > 0 else jnp.zeros((1, N_HEADS * LANES), F32)
    log_lb_all = jnp.log(lb_all)
    log_1mlb_all = jnp.log1p(-lb_all)
    ng = ng_ref[...]
    d3 = d3_ref[...]

    def chunk(ci, carry):
        r0 = pl.multiple_of(ci * c, c)
        for h0 in range(0, N_HEADS, HGRN_HEAD_GROUP):
            heads = range(h0, h0 + HGRN_HEAD_GROUP)
            cols = slice(h0 * LANES, (h0 + HGRN_HEAD_GROUP) * LANES)
            outs = _hgrn_chunk_heads(
                q_ref[0, pl.ds(r0, c), cols], f_ref[0, pl.ds(r0, c), cols],
                i_ref[0, pl.ds(r0, c), cols], g_ref[0, pl.ds(r0, c), cols],
                [st_scr[h] for h in heads], lb_all[:, cols], log_lb_all[:, cols], log_1mlb_all[:, cols],
                ng, d3, amask_ref)
            for h, (y, st_new) in zip(heads, outs):
                st_scr[h] = st_new
                y_ref[0, pl.ds(r0, c), h * LANES:(h + 1) * LANES] = y.astype(BF16)
        return carry

    lax.fori_loop(0, tt // c, chunk, 0)

    @pl.when(t == nt - 1)
    def _():
        for h in range(N_HEADS):
            slast_ref[0, h] = st_scr[h].T


def _hgrn(p3d, s0, gamma, norm_g, layer, tile0):
    bsz, t_len, _ = p3d.shape
    nh = N_HEADS
    d = nh * LANES
    depth = gamma.shape[0]
    tt = min(512, t_len)
    d3, amask = _hgrn_tables()
    body = functools.partial(_hgrn_body, tt=tt, layer=layer)

    def col_spec(off):
        return pl.BlockSpec((1, tt, d), lambda b, t: (b, t, tile0 + off))

    def const_spec(shape):
        return pl.BlockSpec(shape, lambda b, t: (0,) * len(shape))

    return pl.pallas_call(
        body,
        grid=(bsz, t_len // tt),
        in_specs=[
            col_spec(0), col_spec(1), col_spec(2), col_spec(3),
            pl.BlockSpec((1, nh, LANES, LANES), lambda b, t: (b, 0, 0, 0)),
            const_spec((depth, d)),
            const_spec((1, LANES)),
            const_spec(d3.shape), const_spec(amask.shape),
        ],
        out_specs=[
            pl.BlockSpec((1, tt, d), lambda b, t: (b, t, 0)),
            pl.BlockSpec((1, nh, LANES, LANES), lambda b, t: (b, 0, 0, 0)),
        ],
        out_shape=[
            jax.ShapeDtypeStruct((bsz, t_len, d), BF16),
            jax.ShapeDtypeStruct((bsz, nh, LANES, LANES), F32),
        ],
        scratch_shapes=[pltpu.VMEM((nh, LANES, LANES), F32)],
        compiler_params=_cparams(("arbitrary", "arbitrary")),
        name="hgrn",
    )(p3d, p3d, p3d, p3d, s0, gamma, norm_g, d3, amask)


def _merge_body(x_ref, ya_ref, yb_ref, yc_ref, ga_ref, gb_ref, gc_ref, wb_ref, wo_ref, o_ref):
    merged = _sigmoid(ga_ref[...]) * _dot(ya_ref[...], wb_ref[0])
    merged = merged + _sigmoid(gb_ref[...]) * _dot(yb_ref[...], wb_ref[1])
    merged = merged + _sigmoid(gc_ref[...]) * _dot(yc_ref[...], wb_ref[2])
    o_ref[...] = x_ref[...] + _dot(merged.astype(BF16), wo_ref[...])


def _merge(x2d, ya, yb, yc, p2d, w_branch, w_out, gtile0):
    m, d = x2d.shape
    tm = min(512, m)
    row = lambda i: (i, 0)
    return pl.pallas_call(
        _merge_body,
        grid=(m // tm,),
        in_specs=[
            pl.BlockSpec((tm, d), row),
            pl.BlockSpec((tm, d), row), pl.BlockSpec((tm, d), row), pl.BlockSpec((tm, d), row),
            pl.BlockSpec((tm, d), lambda i: (i, gtile0)),
            pl.BlockSpec((tm, d), lambda i: (i, gtile0 + 1)),
            pl.BlockSpec((tm, d), lambda i: (i, gtile0 + 2)),
            pl.BlockSpec((N_BRANCH, d, d), lambda i: (0, 0, 0), pipeline_mode=pl.Buffered(1)),
            pl.BlockSpec((d, d), lambda i: (0, 0), pipeline_mode=pl.Buffered(1)),
        ],
        out_specs=pl.BlockSpec((tm, d), row),
        out_shape=jax.ShapeDtypeStruct((m, d), F32),
        compiler_params=_cparams(("arbitrary",)),
        name="merge",
    )(x2d, ya, yb, yc, p2d, p2d, p2d, w_branch, w_out)


def _gelu_tanh(x):
    return 0.5 * x * (1.0 + jnp.tanh(np.sqrt(2.0 / np.pi).astype(np.float32) * (x + 0.044715 * (x * x * x))))


def _ffn_body(*refs, tm, final):
    if final:
        (x_ref, g_ref, wu_ref, wv_ref, wd_ref, cw_ref, cb_ref, prev_ref, gf_ref,
         o_ref, convnew_ref, y_ref, h_scr, acc_scr, tail_scr) = refs
    else:
        (x_ref, g_ref, wu_ref, wv_ref, wd_ref, cw_ref, cb_ref, prev_ref,
         o_ref, convnew_ref, h_scr, acc_scr, tail_scr) = refs
    t = pl.program_id(1)
    nt = pl.num_programs(1)
    k = pl.program_id(2)
    nk = pl.num_programs(2)

    @pl.when(k == 0)
    def _():
        h_scr[...] = _rms_norm(x_ref[0], g_ref[...]).astype(BF16)
        acc_scr[...] = jnp.zeros_like(acc_scr)

    @pl.when(t == 0)
    def _():
        tail_scr[k] = jnp.zeros(tail_scr.shape[1:], F32)
        tail_scr[k, SUBLANES - (CONV_F - 1):SUBLANES, :] = prev_ref[0]

    h = h_scr[...]
    u = _dot(h, wu_ref[...])
    v = _dot(h, wv_ref[...])
    tail = tail_scr[k]
    uc = cb_ref[...] + cw_ref[CONV_F - 1:CONV_F, :] * u
    for s in range(1, CONV_F):
        uc = uc + cw_ref[CONV_F - 1 - s:CONV_F - s, :] * _shift_rows(u, tail, s)
    tail_scr[k] = u[tm - SUBLANES:tm]
    acc_scr[...] += _dot((_gelu_tanh(uc) * v).astype(BF16), wd_ref[...])

    @pl.when(t == nt - 1)
    def _():
        convnew_ref[0] = u[tm - (CONV_F - 1):tm]

    @pl.when(k == nk - 1)
    def _():
        x_new = x_ref[0] + acc_scr[...]
        o_ref[0] = x_new
        if final:
            y_ref[0] = _rms_norm(x_new, gf_ref[...])


def _ffn(x3d, g, w_up, w_down, conv_w, conv_b, conv_prev, final_g):
    bsz, t_len, d = x3d.shape
    d_ff = w_down.shape[0]
    tm = min(1024, t_len)
    nt = t_len // tm
    kc = 512
    nk = d_ff // kc
    final = final_g is not None
    body = functools.partial(_ffn_body, tm=tm, final=final)
    xspec = pl.BlockSpec((1, tm, d), lambda b, t, k: (b, t, 0))
    in_specs = [
        xspec,
        pl.BlockSpec((1, d), lambda b, t, k: (0, 0)),
        pl.BlockSpec((d, kc), lambda b, t, k: (0, k)),
        pl.BlockSpec((d, kc), lambda b, t, k: (0, nk + k)),
        pl.BlockSpec((kc, d), lambda b, t, k: (k, 0)),
        pl.BlockSpec((CONV_F, kc), lambda b, t, k: (0, k)),
        pl.BlockSpec((1, kc), lambda b, t, k: (0, k)),
        pl.BlockSpec((1, CONV_F - 1, kc), lambda b, t, k: (b, 0, k)),
    ]
    args = [x3d, g, w_up, w_up, w_down, conv_w, conv_b, conv_prev]
    convnew_spec = pl.BlockSpec((1, CONV_F - 1, kc), lambda b, t, k: (b, 0, jnp.where(t == nt - 1, k, 0)))
    out_specs = [xspec, convnew_spec]
    out_shape = [jax.ShapeDtypeStruct((bsz, t_len, d), F32), jax.ShapeDtypeStruct((bsz, CONV_F - 1, d_ff), F32)]
    if final:
        in_specs.append(pl.BlockSpec((1, d), lambda b, t, k: (0, 0)))
        args.append(final_g)
        out_specs.append(xspec)
        out_shape.append(jax.ShapeDtypeStruct((bsz, t_len, d), F32))
    return pl.pallas_call(
        body,
        grid=(bsz, nt, nk),
        in_specs=in_specs,
        out_specs=out_specs,
        out_shape=out_shape,
        scratch_shapes=[pltpu.VMEM((tm, d), BF16), pltpu.VMEM((tm, d), F32), pltpu.VMEM((nk, SUBLANES, kc), F32)],
        compiler_params=_cparams(("arbitrary", "arbitrary", "arbitrary")),
        name="ffn",
    )(*args)


def _lru_gate_weights(gate_w, gate_b):
    n_groups = gate_w.shape[1]
    per = LANES // LRU_GROUP
    nc = n_groups // per
    w = gate_w.reshape(2, nc, per, LRU_GROUP, LRU_GROUP)
    eye = jnp.eye(per, dtype=gate_w.dtype)
    wg = jnp.einsum('gcpij,pq->cpigqj', w, eye).reshape(nc, LANES, 2 * LANES)
    gb = gate_b.reshape(2, nc, LANES).transpose(1, 0, 2).reshape(nc, 1, 2 * LANES)
    return wg.astype(BF16), gb


def _regroup_w_in(w, d):
    tiles = w.reshape(d, -1, d)
    order = np.array([0, 4, 5, 6, 7, 8, 9, 10, 1, 2, 3])
    return tiles[:, order, :].reshape(d, -1).astype(BF16)


def _layer(x3d, conv_a_prev, h0, k_cache, v_cache, s0, conv_f_prev, layer, depth, k_all, v_all, final_g,
           norm_mix_g, w_in, conv_a_w, conv_a_b, lru_gate_w, lru_gate_b, lru_lambda, hgrn_gamma,
           hgrn_norm_g, w_branch, w_out, norm_ffn_g, w_up, conv_f_w, conv_f_b, w_down):
    bsz, t_len, d = x3d.shape
    m = bsz * t_len
    x2d = x3d.reshape(m, d)
    col_scale = jnp.concatenate([jnp.full((1, d), LANES ** -0.5, F32), jnp.ones((1, 2 * d), F32)], axis=1)
    p2d, k_all, v_all, qkv = _proj_in(x2d, norm_mix_g[layer][None], _regroup_w_in(w_in[layer], d), col_scale,
                                      layer, depth, k_all, v_all)
    p3d = p2d.reshape(bsz, t_len, -1)

    wg, gb = _lru_gate_weights(lru_gate_w[layer], lru_gate_b[layer])
    ya, conv_a_new, h_last = _lru(p3d, conv_a_prev, h0[:, None, :], conv_a_w[layer], conv_a_b[layer][None],
                                  wg, gb, lru_lambda[layer][None])

    qkv3d = qkv.reshape(bsz, t_len, N_QKV_TILES * d)
    yb = _attn_fresh(qkv3d) if k_cache is None else _attn_cached(qkv3d, k_cache, v_cache, layer)

    yc, s_last = _hgrn(p3d, s0, hgrn_gamma, hgrn_norm_g[layer][None], layer, 1)

    x1 = _merge(x2d, ya.reshape(m, d), yb.reshape(m, d), yc.reshape(m, d), p2d,
                w_branch[layer].astype(BF16), w_out[layer].astype(BF16), N_P_TILES - N_BRANCH)

    outs = _ffn(x1.reshape(bsz, t_len, d), norm_ffn_g[layer][None], w_up[layer].astype(BF16),
                w_down[layer].astype(BF16), conv_f_w[layer], conv_f_b[layer][None], conv_f_prev, final_g)
    x_new, conv_f_new = outs[0], outs[1]
    y = outs[2] if final_g is not None else None
    return x_new, y, k_all, v_all, (conv_a_new, h_last[:, 0, :], s_last, conv_f_new)


def _run_group(x, conv_a, lru, k_cache, v_cache, hgrn, conv_f, params, final_g, depth):
    bsz, t_len, d = x.shape
    k_all = v_all = y = None
    states = []
    for l in range(depth):
        x, y, k_all, v_all, st = _layer(
            x, conv_a[l], lru[l], k_cache, v_cache, hgrn[l], conv_f[l], l, depth, k_all, v_all,
            final_g if l == depth - 1 else None, *params)
        states.append(st)
    conv_a_new, lru_new, hgrn_new, conv_f_new = (jnp.stack(z) for z in zip(*states))
    kv_shape = (depth, bsz, t_len, N_HEADS, LANES)
    return y, conv_a_new, lru_new, k_all.reshape(kv_shape), v_all.reshape(kv_shape), hgrn_new, conv_f_new


def kernel(x_prompt, x_sample, cache_conv_a, state_lru, cache_k, cache_v, state_hgrn, cache_conv_ffn, norm_mix_g, w_in, conv_a_w, conv_a_b, lru_gate_w, lru_gate_b, lru_lambda, hgrn_gamma, hgrn_norm_g, w_branch, w_out, norm_ffn_g, w_up, conv_f_w, conv_f_b, w_down, norm_final_g):
    depth = w_in.shape[0]
    d = x_prompt.shape[-1]
    d_ff = w_down.shape[1]
    assert d == N_HEADS * LANES and d_ff == 3 * d and w_in.shape[2] == (N_P_TILES + N_QKV_TILES) * d
    params = (norm_mix_g, w_in, conv_a_w, conv_a_b, lru_gate_w, lru_gate_b, lru_lambda, hgrn_gamma,
              hgrn_norm_g, w_branch, w_out, norm_ffn_g, w_up, conv_f_w, conv_f_b, w_down)
    final_g = norm_final_g[None]

    bp = x_prompt.shape[0]
    zeros = lambda *shape: [jnp.zeros(shape, F32)] * depth
    (y_prompt, p_conv_a, p_lru, p_k, p_v, p_hgrn, p_conv_ffn) = _run_group(
        x_prompt, zeros(bp, CONV_A - 1, d), zeros(bp, d), None, None,
        zeros(bp, N_HEADS, LANES, LANES), zeros(bp, CONV_F - 1, d_ff), params, final_g, depth)
    (y_sample, s_conv_a, s_lru, s_k, s_v, s_hgrn, s_conv_ffn) = _run_group(
        x_sample, cache_conv_a, state_lru, cache_k, cache_v, state_hgrn, cache_conv_ffn, params, final_g, depth)
    return (y_prompt, y_sample, p_conv_a, s_conv_a, p_lru, s_lru, p_k, s_k, p_v, s_v,
            p_hgrn, s_hgrn, p_conv_ffn, s_conv_ffn)
```

```python
import functools

import numpy as np
import jax
import jax.numpy as jnp
from jax import lax
from jax.experimental import pallas as pl
from jax.experimental.pallas import tpu as pltpu

F32 = jnp.float32
BF16 = jnp.bfloat16

EPS = 1e-6
LANES = 128
SUBLANES = 8
VMEM_LIMIT = 48 * 1024 * 1024

N_HEADS = 8
LRU_GROUP = 64
C_LRU = 8.0
CONV_A = 4
CONV_F = 3
N_BRANCH = 3
HGRN_CHUNK = 64
HGRN_LEVELS = 6
HGRN_HEAD_GROUP = 4
ATT_BLOCK = 256
ATT_STREAMS = 8
ATT_LOG_CUTOFF = -104.0
ATT_LOG_DEAD = -1e30


def _cparams(semantics):
    return pltpu.CompilerParams(dimension_semantics=semantics, vmem_limit_bytes=VMEM_LIMIT)


def _sigmoid(x):
    return 1.0 / (1.0 + jnp.exp(-x))


def _softplus(x):
    return jnp.maximum(x, 0.0) + jnp.log1p(jnp.exp(-jnp.abs(x)))


def _rms_norm(x, g):
    ms = jnp.mean(x * x, axis=-1, keepdims=True)
    return x * lax.rsqrt(ms + EPS) * g


def _dot(a, b):
    return jnp.dot(a, b, preferred_element_type=F32)


def _dot_nt(a, b):
    return lax.dot_general(a, b, (((1,), (1,)), ((), ())), preferred_element_type=F32)


def _dot_tn(a, b):
    return lax.dot_general(a, b, (((0,), (0,)), ((), ())), preferred_element_type=F32)


def _shift_rows(x, tail, s):
    n_rows = x.shape[0]
    rows = lax.broadcasted_iota(jnp.int32, (SUBLANES, x.shape[1]), 0)
    rolled = pltpu.roll(x, s, 0)
    first = jnp.where(rows < s, pltpu.roll(tail, s, 0), rolled[0:SUBLANES])
    if n_rows == SUBLANES:
        return first
    return jnp.concatenate([first, rolled[SUBLANES:]], axis=0)


N_P_TILES = 8
N_QKV_TILES = 3


def _w_in_tile(j):
    return jnp.where(j == 0, 0, jnp.where(j < N_P_TILES, j + N_QKV_TILES, j - (N_P_TILES - 1)))


def _proj_in_body(*refs, aliased):
    if aliased:
        x_ref, g_ref, w_ref, cs_ref, _, _, p_ref, k_ref, v_ref, qkv_ref, h_scr = refs
    else:
        x_ref, g_ref, w_ref, cs_ref, p_ref, k_ref, v_ref, qkv_ref, h_scr = refs
    j = pl.program_id(1)

    @pl.when(j == 0)
    def _():
        h_scr[...] = _rms_norm(x_ref[...], g_ref[...]).astype(BF16)

    @pl.when(j < N_P_TILES)
    def _():
        p_ref[...] = _dot(h_scr[...], w_ref[...])

    @pl.when(j == N_P_TILES)
    def _():
        qkv_ref[...] = (_dot(h_scr[...], w_ref[...]) * cs_ref[...]).astype(BF16)

    for off, dst_ref in ((1, k_ref), (2, v_ref)):
        @pl.when(j == N_P_TILES + off)
        def _(dst_ref=dst_ref):
            acc = _dot(h_scr[...], w_ref[...])
            dst_ref[...] = acc
            qkv_ref[...] = (acc * cs_ref[...]).astype(BF16)


def _proj_in(x2d, g, w_bf16, col_scale, layer, depth, k_all, v_all):
    m, d = x2d.shape
    n_tiles = w_bf16.shape[1] // d
    assert n_tiles == N_P_TILES + N_QKV_TILES
    tm = min(1024, m)
    aliased = k_all is not None
    qkv_col = lambda i, j: (i, jnp.clip(j - N_P_TILES, 0, N_QKV_TILES - 1))
    in_specs = [
        pl.BlockSpec((tm, d), lambda i, j: (i, 0)),
        pl.BlockSpec((1, d), lambda i, j: (0, 0)),
        pl.BlockSpec((d, d), lambda i, j: (0, _w_in_tile(j))),
        pl.BlockSpec((1, d), lambda i, j: (0, jnp.clip(j - N_P_TILES, 0, N_QKV_TILES - 1))),
    ]
    args = [x2d, g, w_bf16, col_scale]
    aliases = {}
    if aliased:
        in_specs += [pl.BlockSpec(memory_space=pl.ANY), pl.BlockSpec(memory_space=pl.ANY)]
        args += [k_all, v_all]
        aliases = {4: 1, 5: 2}
    kv_spec = pl.BlockSpec((None, tm, d), lambda i, j: (layer, i, 0))
    return pl.pallas_call(
        functools.partial(_proj_in_body, aliased=aliased),
        grid=(m // tm, n_tiles),
        in_specs=in_specs,
        out_specs=[
            pl.BlockSpec((tm, d), lambda i, j: (i, jnp.minimum(j, N_P_TILES - 1))),
            kv_spec, kv_spec,
            pl.BlockSpec((tm, d), qkv_col),
        ],
        out_shape=[
            jax.ShapeDtypeStruct((m, N_P_TILES * d), F32),
            jax.ShapeDtypeStruct((depth, m, d), F32),
            jax.ShapeDtypeStruct((depth, m, d), F32),
            jax.ShapeDtypeStruct((m, N_QKV_TILES * d), BF16),
        ],
        scratch_shapes=[pltpu.VMEM((tm, d), BF16)],
        input_output_aliases=aliases,
        compiler_params=_cparams(("arbitrary", "arbitrary")),
        name="proj_in",
    )(*args)


def _lru_body(x_ref, prev_ref, h0_ref, cw_ref, cb_ref, wg_ref, gb_ref, lam_ref,
              ya_ref, convnew_ref, hlast_ref, tail_scr, h_scr, *, tt):
    t = pl.program_id(2)
    nt = pl.num_programs(2)

    @pl.when(t == 0)
    def _():
        tail_scr[...] = jnp.zeros_like(tail_scr)
        tail_scr[SUBLANES - (CONV_A - 1):SUBLANES, :] = prev_ref[0]
        h_scr[...] = h0_ref[0]

    x = x_ref[0]
    tail = tail_scr[...]
    y = cb_ref[...] + cw_ref[CONV_A - 1:CONV_A, :] * x
    for s in range(1, CONV_A):
        y = y + cw_ref[CONV_A - 1 - s:CONV_A - s, :] * _shift_rows(x, tail, s)
    tail_scr[...] = x_ref[0, tt - SUBLANES:tt, :]

    gates = _dot(y.astype(BF16), wg_ref[0]) + gb_ref[0]
    r = _sigmoid(gates[:, :LANES])
    i = _sigmoid(gates[:, LANES:])
    log_a = (-C_LRU) * r * _softplus(-lam_ref[...])
    a = jnp.exp(log_a)
    th = jnp.tanh(log_a)
    b = jnp.sqrt(-2.0 * th / (1.0 - th)) * (i * y)

    row = lax.broadcasted_iota(jnp.int32, (tt, LANES), 0)
    b = b + jnp.where(row == 0, a * h_scr[...], 0.0)
    s = 1
    while s < tt:
        if s < SUBLANES:
            a_sh = jnp.where(row < s, 1.0, pltpu.roll(a, s, 0))
            b_sh = jnp.where(row < s, 0.0, pltpu.roll(b, s, 0))
        else:
            a_sh = jnp.concatenate([jnp.ones((s, LANES), F32), a[:tt - s]], axis=0)
            b_sh = jnp.concatenate([jnp.zeros((s, LANES), F32), b[:tt - s]], axis=0)
        b = a * b_sh + b
        a = a * a_sh
        s *= 2
    h_last = b[tt - 1:tt]
    ya_ref[0] = b.astype(BF16)
    h_scr[...] = h_last

    @pl.when(t == nt - 1)
    def _():
        convnew_ref[0] = x_ref[0, tt - (CONV_A - 1):tt, :]
        hlast_ref[0] = h_last


def _lru(p3d, conv_prev, h0, conv_w, conv_b, wg, gb, lam):
    bsz, t_len, _ = p3d.shape
    d_a = conv_w.shape[1]
    nc = d_a // LANES
    tt = min(1024, t_len)
    body = functools.partial(_lru_body, tt=tt)
    return pl.pallas_call(
        body,
        grid=(bsz, nc, t_len // tt),
        in_specs=[
            pl.BlockSpec((1, tt, LANES), lambda b, c, t: (b, t, c)),
            pl.BlockSpec((1, CONV_A - 1, LANES), lambda b, c, t: (b, 0, c)),
            pl.BlockSpec((1, 1, LANES), lambda b, c, t: (b, 0, c)),
            pl.BlockSpec((CONV_A, LANES), lambda b, c, t: (0, c)),
            pl.BlockSpec((1, LANES), lambda b, c, t: (0, c)),
            pl.BlockSpec((1, LANES, 2 * LANES), lambda b, c, t: (c, 0, 0)),
            pl.BlockSpec((1, 1, 2 * LANES), lambda b, c, t: (c, 0, 0)),
            pl.BlockSpec((1, LANES), lambda b, c, t: (0, c)),
        ],
        out_specs=[
            pl.BlockSpec((1, tt, LANES), lambda b, c, t: (b, t, c)),
            pl.BlockSpec((1, CONV_A - 1, LANES), lambda b, c, t: (b, 0, c)),
            pl.BlockSpec((1, 1, LANES), lambda b, c, t: (b, 0, c)),
        ],
        out_shape=[
            jax.ShapeDtypeStruct((bsz, t_len, d_a), BF16),
            jax.ShapeDtypeStruct((bsz, CONV_A - 1, d_a), F32),
            jax.ShapeDtypeStruct((bsz, 1, d_a), F32),
        ],
        scratch_shapes=[pltpu.VMEM((SUBLANES, LANES), F32), pltpu.VMEM((1, LANES), F32)],
        compiler_params=_cparams(("arbitrary", "arbitrary", "arbitrary")),
        name="lru",
    )(p3d, conv_prev, h0, conv_w, conv_b, wg, gb, lam)


def _suffix_sum(l, u2):
    hi = l.astype(BF16)
    lo = (l - hi.astype(F32)).astype(BF16)
    return _dot(jnp.concatenate([hi, lo], axis=1), u2)


def _attn_blocks(qs, kvs, u2, accs, carries, mask):
    zs = [_dot_nt(q, kb) for q, (kb, _) in zip(qs, kvs)]
    sps = [jnp.maximum(z, 0.0) + jnp.log(1.0 + jnp.exp(-jnp.abs(z))) for z in zs]
    ls = [-sp if mask is None else jnp.where(mask, -sp, 0.0) for sp in sps]
    sufs = [_suffix_sum(l, u2) for l in ls]
    ws = [jnp.exp(z - sp + (suf + carry)) for z, sp, suf, carry in zip(zs, sps, sufs, carries)]
    if mask is not None:
        ws = [jnp.where(mask, w, 0.0) for w in ws]
    accs = [acc + _dot(w.astype(BF16), vb) for acc, w, (_, vb) in zip(accs, ws, kvs)]
    return accs, [jnp.sum(l, axis=1, keepdims=True) for l in ls]


def _max_log_mass(states, n_left):
    m = None
    for nl, (_, carry) in zip(n_left, states):
        c = jnp.where(nl > 0, carry, ATT_LOG_DEAD)
        m = c if m is None else jnp.maximum(m, c)
    return jnp.max(m)


def _walk_back(qs, states, n_blocks, loads, u2):
    def alive(jj, sts):
        return (_max_log_mass(sts, [nb - jj for nb in n_blocks]) > ATT_LOG_CUTOFF).astype(jnp.int32)

    def cond(st):
        return st[1] > 0

    def body(st):
        jj, _, sts = st
        valids = [nb - 1 - jj >= 0 for nb in n_blocks]
        kvs = [load(jnp.maximum(nb - 1 - jj, 0)) for nb, load in zip(n_blocks, loads)]
        accs, row_sums = _attn_blocks(
            qs, kvs, u2, [acc for acc, _ in sts],
            [jnp.where(valid, carry, ATT_LOG_DEAD) for valid, (_, carry) in zip(valids, sts)], None)
        new = [(acc, jnp.where(valid, carry + row_sum, carry))
               for acc, valid, row_sum, (_, carry) in zip(accs, valids, row_sums, sts)]
        jj = jj + 1
        return jj, alive(jj, new), new

    jj0 = jnp.int32(0)
    return lax.while_loop(cond, body, (jj0, alive(jj0, states), states))[2]


def _diag_blocks(qs, kvs, u2):
    tq = qs[0].shape[0]
    row = lax.broadcasted_iota(jnp.int32, (tq, tq), 0)
    col = lax.broadcasted_iota(jnp.int32, (tq, tq), 1)
    accs, row_sums = _attn_blocks(qs, kvs, u2, [jnp.zeros((tq, LANES), F32)] * len(qs),
                                  [jnp.zeros((tq, 1), F32)] * len(qs), col < row)
    return list(zip(accs, row_sums))


def _attn_fresh_body(q_ref, k_ref, v_ref, u_ref, o_ref, *, tq, n_streams):
    i = pl.program_id(2)
    u2 = u_ref[...]

    def load(j):
        start = pl.multiple_of(j * tq, tq)
        return k_ref[0, pl.ds(start, tq), :], v_ref[0, pl.ds(start, tq), :]

    n_prev = [i * n_streams + s for s in range(n_streams)]
    qs = [q_ref[0, s * tq:(s + 1) * tq, :] for s in range(n_streams)]
    states = _diag_blocks(qs, [load(qi) for qi in n_prev], u2)
    states = _walk_back(qs, states, n_prev, [load] * n_streams, u2)
    for s in range(n_streams):
        o_ref[0, s * tq:(s + 1) * tq, :] = states[s][0].astype(BF16)


def _strict_upper2(n):
    u = np.triu(np.ones((n, n), np.float32), 1).T
    return jnp.asarray(np.concatenate([u, u], axis=0), dtype=BF16)


def _attn_fresh(qkv3d):
    bsz, t_len, d3 = qkv3d.shape
    d_b = d3 // 3
    nh = d_b // LANES
    tq = min(ATT_BLOCK, t_len)
    n_streams = min(ATT_STREAMS, t_len // tq)
    tstep = tq * n_streams
    body = functools.partial(_attn_fresh_body, tq=tq, n_streams=n_streams)
    return pl.pallas_call(
        body,
        grid=(bsz, nh, t_len // tstep),
        in_specs=[
            pl.BlockSpec((1, tstep, LANES), lambda b, h, i: (b, i, h)),
            pl.BlockSpec((1, t_len, LANES), lambda b, h, i: (b, 0, nh + h)),
            pl.BlockSpec((1, t_len, LANES), lambda b, h, i: (b, 0, 2 * nh + h)),
            pl.BlockSpec((2 * tq, tq), lambda b, h, i: (0, 0)),
        ],
        out_specs=pl.BlockSpec((1, tstep, LANES), lambda b, h, i: (b, i, h)),
        out_shape=jax.ShapeDtypeStruct((bsz, t_len, d_b), BF16),
        compiler_params=_cparams(("arbitrary", "arbitrary", "arbitrary")),
        name="attn_fresh",
    )(qkv3d, qkv3d, qkv3d, _strict_upper2(tq))


def _attn_cached_body(q_ref, k_ref, v_ref, u_ref, kp_ref, vp_ref, up_ref, o_ref, acc_scr, carry_scr, alive_scr,
                      *, tkp, n_sub):
    p = pl.program_id(1)
    n_chunks = pl.num_programs(1)
    heads = range(N_HEADS)
    qs = [q_ref[0, :, h * LANES:(h + 1) * LANES] for h in heads]

    @pl.when(p == 0)
    def _():
        cols = [slice(h * LANES, (h + 1) * LANES) for h in heads]
        states = _diag_blocks(qs, [(k_ref[0, :, c], v_ref[0, :, c]) for c in cols], u_ref[...])
        for h in heads:
            acc_scr[h] = states[h][0]
            carry_scr[h] = states[h][1]
        alive_scr[0] = 1

    @pl.when(alive_scr[0] > 0)
    def _():
        def load_head(h):
            def load(j):
                rows = pl.ds(j * (tkp * N_HEADS) + h, tkp, stride=N_HEADS)
                return kp_ref[0, rows, :].astype(BF16), vp_ref[0, rows, :].astype(BF16)
            return load

        states = [(acc_scr[h], carry_scr[h]) for h in heads]
        states = _walk_back(qs, states, [n_sub] * N_HEADS, [load_head(h) for h in heads], up_ref[...])
        for h in heads:
            acc_scr[h] = states[h][0]
            carry_scr[h] = states[h][1]
        alive_scr[0] = (_max_log_mass(states, [1] * N_HEADS) > ATT_LOG_CUTOFF).astype(jnp.int32)

    @pl.when(p == n_chunks - 1)
    def _():
        for h in heads:
            o_ref[0, :, h * LANES:(h + 1) * LANES] = acc_scr[h].astype(BF16)


def _attn_cached(qkv3d, k_cache, v_cache, layer):
    bsz, t_len, d3 = qkv3d.shape
    d = d3 // 3
    depth, _, past, nh, dh = k_cache.shape
    assert nh == N_HEADS and dh == LANES and d == nh * dh and t_len <= ATT_BLOCK
    tkp = min(ATT_BLOCK, past)
    n_sub = max(1, 512 // tkp)
    tpc = n_sub * tkp
    n_chunks = past // tpc
    kc = k_cache.reshape(depth, bsz, past * nh, dh)
    vc = v_cache.reshape(depth, bsz, past * nh, dh)
    new_spec = lambda tile: pl.BlockSpec((1, t_len, d), lambda b, p: (b, 0, tile))
    past_spec = pl.BlockSpec((None, 1, tpc * nh, dh), lambda b, p: (layer, b, n_chunks - 1 - p, 0))
    body = functools.partial(_attn_cached_body, tkp=tkp, n_sub=n_sub)
    return pl.pallas_call(
        body,
        grid=(bsz, n_chunks),
        in_specs=[
            new_spec(0), new_spec(1), new_spec(2),
            pl.BlockSpec((2 * t_len, t_len), lambda b, p: (0, 0)),
            past_spec, past_spec,
            pl.BlockSpec((2 * tkp, tkp), lambda b, p: (0, 0)),
        ],
        out_specs=pl.BlockSpec((1, t_len, d), lambda b, p: (b, 0, 0)),
        out_shape=jax.ShapeDtypeStruct((bsz, t_len, d), BF16),
        scratch_shapes=[pltpu.VMEM((nh, t_len, LANES), F32), pltpu.VMEM((nh, t_len, 1), F32),
                        pltpu.SMEM((1,), jnp.int32)],
        compiler_params=_cparams(("arbitrary", "arbitrary")),
        name="attn_cached",
    )(qkv3d, qkv3d, qkv3d, _strict_upper2(t_len), kc, vc, _strict_upper2(tkp))


def _hgrn_tables():
    c = HGRN_CHUNK
    n_seg = 2 + HGRN_LEVELS
    d = np.zeros((n_seg * c, c), np.float32)
    jj = np.arange(c)
    amask = np.zeros((HGRN_LEVELS + 1, c, c), np.float32)
    amask[HGRN_LEVELS] = np.eye(c)
    for r in range(c):
        d[r] = jj <= r
        d[c + r] = jj > r
    for li in range(HGRN_LEVELS):
        m = c >> (li + 1)
        for r in range(c):
            blk, pos = divmod(r, 2 * m)
            boundary = blk * 2 * m + m - 1
            if pos >= m:
                d[(2 + li) * c + r] = (jj > boundary) & (jj <= r)
            else:
                d[(2 + li) * c + r] = (jj > r) & (jj <= boundary)
        late = (jj % (2 * m)) >= m
        blk_id = jj // (2 * m)
        amask[li] = (blk_id[:, None] == blk_id[None, :]) & late[:, None] & ~late[None, :]
    d3 = np.concatenate([d, d, d], axis=1)
    gmask = np.stack([np.kron(np.eye(HGRN_HEAD_GROUP, dtype=np.float32), a) for a in amask])
    return jnp.asarray(d3, dtype=BF16), jnp.asarray(gmask)


def _hgrn_chunk_heads(qr, fr, v, gr, sts, lb, log_lb, log_1mlb, ng, d3, amask_ref):
    c = HGRN_CHUNK
    q = qr * _sigmoid(qr)
    ls = jnp.minimum(fr, 0.0) - jnp.log1p(jnp.exp(-jnp.abs(fr)))
    x2 = log_1mlb + ls
    mx = jnp.maximum(log_lb, x2)
    mn = jnp.minimum(log_lb, x2)
    lf = mx + jnp.log1p(jnp.exp(mn - mx))
    kk = (1.0 - lb) * _sigmoid(-fr)

    hi = lf.astype(BF16)
    r1 = lf - hi.astype(F32)
    mid = r1.astype(BF16)
    lo = (r1 - mid.astype(F32)).astype(BF16)
    ex = jnp.exp(_dot(d3, jnp.concatenate([hi, mid, lo], axis=0)))
    xb = ex[0:c]
    qin = (q * xb).astype(BF16)
    kr = (kk * ex[c:2 * c]).astype(BF16)
    qb, kb, vb = q.astype(BF16), kk.astype(BF16), v.astype(BF16)
    qms = [(q * ex[(2 + li) * c:(3 + li) * c]).astype(BF16) for li in range(HGRN_LEVELS)]
    kms = [(kk * ex[(2 + li) * c:(3 + li) * c]).astype(BF16) for li in range(HGRN_LEVELS)]
    gate = gr * _sigmoid(gr)

    def rows(x):
        return jnp.concatenate([x[:, h * LANES:(h + 1) * LANES] for h in range(len(sts))], axis=0)

    att = amask_ref[HGRN_LEVELS] * _dot_nt(rows(qb), rows(kb))
    for li in range(HGRN_LEVELS):
        att = att + amask_ref[li] * _dot_nt(rows(qms[li]), rows(kms[li]))
    o_intra = _dot(att.astype(BF16), rows(vb))

    outs = []
    for h, st in enumerate(sts):
        cols = slice(h * LANES, (h + 1) * LANES)
        o = _dot_nt(qin[:, cols], st.astype(BF16)) + o_intra[h * c:(h + 1) * c]
        st_new = st * xb[c - 1:c, cols] + _dot_tn(vb[:, cols], kr[:, cols])
        outs.append((_rms_norm(o, ng) * gate[:, cols], st_new))
    return outs


def _hgrn_body(q_ref, f_ref, i_ref, g_ref, s0_ref, gamma_ref, ng_ref, d3_ref, amask_ref,
               y_ref, slast_ref, st_scr, *, tt, layer):
    t = pl.program_id(1)
    nt = pl.num_programs(1)
    c = HGRN_CHUNK

    @pl.when(t == 0)
    def _():
        for h in range(N_HEADS):
            st_scr[h] = s0_ref[0, h].T

    gam = gamma_ref[...]
    e = jnp.exp(gam - jnp.max(gam, axis=0, keepdims=True))
    lb_all = jnp.sum(e[1:layer + 1], axis=0, keepdims=True) / jnp.sum(e, axis=0, keepdims=True) \
        if layer > 0 else jnp.zeros((1, N_HEADS * LANES), F32)
    log_lb_all = jnp.log(lb_all)
    log_1mlb_all = jnp.log1p(-lb_all)
    ng = ng_ref[...]
    d3 = d3_ref[...]

    def chunk(ci, carry):
        r0 = pl.multiple_of(ci * c, c)
        for h0 in range(0, N_HEADS, HGRN_HEAD_GROUP):
            heads = range(h0, h0 + HGRN_HEAD_GROUP)
            cols = slice(h0 * LANES, (h0 + HGRN_HEAD_GROUP) * LANES)
            outs = _hgrn_chunk_heads(
                q_ref[0, pl.ds(r0, c), cols], f_ref[0, pl.ds(r0, c), cols],
                i_ref[0, pl.ds(r0, c), cols], g_ref[0, pl.ds(r0, c), cols],
                [st_scr[h] for h in heads], lb_all[:, cols], log_lb_all[:, cols], log_1mlb_all[:, cols],
                ng, d3, amask_ref)
            for h, (y, st_new) in zip(heads, outs):
                st_scr[h] = st_new
                y_ref[0, pl.ds(r0, c), h * LANES:(h + 1) * LANES] = y.astype(BF16)
        return carry

    lax.fori_loop(0, tt // c, chunk, 0)

    @pl.when(t == nt - 1)
    def _():
        for h in range(N_HEADS):
            slast_ref[0, h] = st_scr[h].T


def _hgrn(p3d, s0, gamma, norm_g, layer, tile0):
    bsz, t_len, _ = p3d.shape
    nh = N_HEADS
    d = nh * LANES
    depth = gamma.shape[0]
    tt = min(512, t_len)
    d3, amask = _hgrn_tables()
    body = functools.partial(_hgrn_body, tt=tt, layer=layer)

    def col_spec(off):
        return pl.BlockSpec((1, tt, d), lambda b, t: (b, t, tile0 + off))

    def const_spec(shape):
        return pl.BlockSpec(shape, lambda b, t: (0,) * len(shape))

    return pl.pallas_call(
        body,
        grid=(bsz, t_len // tt),
        in_specs=[
            col_spec(0), col_spec(1), col_spec(2), col_spec(3),
            pl.BlockSpec((1, nh, LANES, LANES), lambda b, t: (b, 0, 0, 0)),
            const_spec((depth, d)),
            const_spec((1, LANES)),
            const_spec(d3.shape), const_spec(amask.shape),
        ],
        out_specs=[
            pl.BlockSpec((1, tt, d), lambda b, t: (b, t, 0)),
            pl.BlockSpec((1, nh, LANES, LANES), lambda b, t: (b, 0, 0, 0)),
        ],
        out_shape=[
            jax.ShapeDtypeStruct((bsz, t_len, d), BF16),
            jax.ShapeDtypeStruct((bsz, nh, LANES, LANES), F32),
        ],
        scratch_shapes=[pltpu.VMEM((nh, LANES, LANES), F32)],
        compiler_params=_cparams(("arbitrary", "arbitrary")),
        name="hgrn",
    )(p3d, p3d, p3d, p3d, s0, gamma, norm_g, d3, amask)


def _merge_body(x_ref, ya_ref, yb_ref, yc_ref, ga_ref, gb_ref, gc_ref, wb_ref, wo_ref, o_ref):
    merged = _sigmoid(ga_ref[...]) * _dot(ya_ref[...], wb_ref[0])
    merged = merged + _sigmoid(gb_ref[...]) * _dot(yb_ref[...], wb_ref[1])
    merged = merged + _sigmoid(gc_ref[...]) * _dot(yc_ref[...], wb_ref[2])
    o_ref[...] = x_ref[...] + _dot(merged.astype(BF16), wo_ref[...])


def _merge(x2d, ya, yb, yc, p2d, w_branch, w_out, gtile0):
    m, d = x2d.shape
    tm = min(512, m)
    row = lambda i: (i, 0)
    return pl.pallas_call(
        _merge_body,
        grid=(m // tm,),
        in_specs=[
            pl.BlockSpec((tm, d), row),
            pl.BlockSpec((tm, d), row), pl.BlockSpec((tm, d), row), pl.BlockSpec((tm, d), row),
            pl.BlockSpec((tm, d), lambda i: (i, gtile0)),
            pl.BlockSpec((tm, d), lambda i: (i, gtile0 + 1)),
            pl.BlockSpec((tm, d), lambda i: (i, gtile0 + 2)),
            pl.BlockSpec((N_BRANCH, d, d), lambda i: (0, 0, 0), pipeline_mode=pl.Buffered(1)),
            pl.BlockSpec((d, d), lambda i: (0, 0), pipeline_mode=pl.Buffered(1)),
        ],
        out_specs=pl.BlockSpec((tm, d), row),
        out_shape=jax.ShapeDtypeStruct((m, d), F32),
        compiler_params=_cparams(("arbitrary",)),
        name="merge",
    )(x2d, ya, yb, yc, p2d, p2d, p2d, w_branch, w_out)


def _gelu_tanh(x):
    return 0.5 * x * (1.0 + jnp.tanh(np.sqrt(2.0 / np.pi).astype(np.float32) * (x + 0.044715 * (x * x * x))))


def _ffn_body(*refs, tm, final):
    if final:
        (x_ref, g_ref, wu_ref, wv_ref, wd_ref, cw_ref, cb_ref, prev_ref, gf_ref,
         o_ref, convnew_ref, y_ref, h_scr, acc_scr, tail_scr) = refs
    else:
        (x_ref, g_ref, wu_ref, wv_ref, wd_ref, cw_ref, cb_ref, prev_ref,
         o_ref, convnew_ref, h_scr, acc_scr, tail_scr) = refs
    t = pl.program_id(1)
    nt = pl.num_programs(1)
    k = pl.program_id(2)
    nk = pl.num_programs(2)

    @pl.when(k == 0)
    def _():
        h_scr[...] = _rms_norm(x_ref[0], g_ref[...]).astype(BF16)
        acc_scr[...] = jnp.zeros_like(acc_scr)

    @pl.when(t == 0)
    def _():
        tail_scr[k] = jnp.zeros(tail_scr.shape[1:], F32)
        tail_scr[k, SUBLANES - (CONV_F - 1):SUBLANES, :] = prev_ref[0]

    h = h_scr[...]
    u = _dot(h, wu_ref[...])
    v = _dot(h, wv_ref[...])
    tail = tail_scr[k]
    uc = cb_ref[...] + cw_ref[CONV_F - 1:CONV_F, :] * u
    for s in range(1, CONV_F):
        uc = uc + cw_ref[CONV_F - 1 - s:CONV_F - s, :] * _shift_rows(u, tail, s)
    tail_scr[k] = u[tm - SUBLANES:tm]
    acc_scr[...] += _dot((_gelu_tanh(uc) * v).astype(BF16), wd_ref[...])

    @pl.when(t == nt - 1)
    def _():
        convnew_ref[0] = u[tm - (CONV_F - 1):tm]

    @pl.when(k == nk - 1)
    def _():
        x_new = x_ref[0] + acc_scr[...]
        o_ref[0] = x_new
        if final:
            y_ref[0] = _rms_norm(x_new, gf_ref[...])


def _ffn(x3d, g, w_up, w_down, conv_w, conv_b, conv_prev, final_g):
    bsz, t_len, d = x3d.shape
    d_ff = w_down.shape[0]
    tm = min(1024, t_len)
    nt = t_len // tm
    kc = 512
    nk = d_ff // kc
    final = final_g is not None
    body = functools.partial(_ffn_body, tm=tm, final=final)
    xspec = pl.BlockSpec((1, tm, d), lambda b, t, k: (b, t, 0))
    in_specs = [
        xspec,
        pl.BlockSpec((1, d), lambda b, t, k: (0, 0)),
        pl.BlockSpec((d, kc), lambda b, t, k: (0, k)),
        pl.BlockSpec((d, kc), lambda b, t, k: (0, nk + k)),
        pl.BlockSpec((kc, d), lambda b, t, k: (k, 0)),
        pl.BlockSpec((CONV_F, kc), lambda b, t, k: (0, k)),
        pl.BlockSpec((1, kc), lambda b, t, k: (0, k)),
        pl.BlockSpec((1, CONV_F - 1, kc), lambda b, t, k: (b, 0, k)),
    ]
    args = [x3d, g, w_up, w_up, w_down, conv_w, conv_b, conv_prev]
    convnew_spec = pl.BlockSpec((1, CONV_F - 1, kc), lambda b, t, k: (b, 0, jnp.where(t == nt - 1, k, 0)))
    out_specs = [xspec, convnew_spec]
    out_shape = [jax.ShapeDtypeStruct((bsz, t_len, d), F32), jax.ShapeDtypeStruct((bsz, CONV_F - 1, d_ff), F32)]
    if final:
        in_specs.append(pl.BlockSpec((1, d), lambda b, t, k: (0, 0)))
        args.append(final_g)
        out_specs.append(xspec)
        out_shape.append(jax.ShapeDtypeStruct((bsz, t_len, d), F32))
    return pl.pallas_call(
        body,
        grid=(bsz, nt, nk),
        in_specs=in_specs,
        out_specs=out_specs,
        out_shape=out_shape,
        scratch_shapes=[pltpu.VMEM((tm, d), BF16), pltpu.VMEM((tm, d), F32), pltpu.VMEM((nk, SUBLANES, kc), F32)],
        compiler_params=_cparams(("arbitrary", "arbitrary", "arbitrary")),
        name="ffn",
    )(*args)


def _lru_gate_weights(gate_w, gate_b):
    n_groups = gate_w.shape[1]
    per = LANES // LRU_GROUP
    nc = n_groups // per
    w = gate_w.reshape(2, nc, per, LRU_GROUP, LRU_GROUP)
    eye = jnp.eye(per, dtype=gate_w.dtype)
    wg = jnp.einsum('gcpij,pq->cpigqj', w, eye).reshape(nc, LANES, 2 * LANES)
    gb = gate_b.reshape(2, nc, LANES).transpose(1, 0, 2).reshape(nc, 1, 2 * LANES)
    return wg.astype(BF16), gb


def _layer(x3d, conv_a_prev, h0, k_cache, v_cache, s0, conv_f_prev, layer, depth, k_all, v_all, final_g,
           norm_mix_g, w_in, conv_a_w, conv_a_b, lru_gate_w, lru_gate_b, lru_lambda, hgrn_gamma,
           hgrn_norm_g, w_branch, w_out, norm_ffn_g, w_up, conv_f_w, conv_f_b, w_down):
    bsz, t_len, d = x3d.shape
    m = bsz * t_len
    x2d = x3d.reshape(m, d)
    col_scale = jnp.concatenate([jnp.full((1, d), LANES ** -0.5, F32), jnp.ones((1, 2 * d), F32)], axis=1)
    p2d, k_all, v_all, qkv = _proj_in(x2d, norm_mix_g[layer][None], w_in[layer].astype(BF16), col_scale,
                                      layer, depth, k_all, v_all)
    p3d = p2d.reshape(bsz, t_len, -1)

    wg, gb = _lru_gate_weights(lru_gate_w[layer], lru_gate_b[layer])
    ya, conv_a_new, h_last = _lru(p3d, conv_a_prev, h0[:, None, :], conv_a_w[layer], conv_a_b[layer][None],
                                  wg, gb, lru_lambda[layer][None])

    qkv3d = qkv.reshape(bsz, t_len, N_QKV_TILES * d)
    yb = _attn_fresh(qkv3d) if k_cache is None else _attn_cached(qkv3d, k_cache, v_cache, layer)

    yc, s_last = _hgrn(p3d, s0, hgrn_gamma, hgrn_norm_g[layer][None], layer, 1)

    x1 = _merge(x2d, ya.reshape(m, d), yb.reshape(m, d), yc.reshape(m, d), p2d,
                w_branch[layer].astype(BF16), w_out[layer].astype(BF16), N_P_TILES - N_BRANCH)

    outs = _ffn(x1.reshape(bsz, t_len, d), norm_ffn_g[layer][None], w_up[layer].astype(BF16),
                w_down[layer].astype(BF16), conv_f_w[layer], conv_f_b[layer][None], conv_f_prev, final_g)
    x_new, conv_f_new = outs[0], outs[1]
    y = outs[2] if final_g is not None else None
    return x_new, y, k_all, v_all, (conv_a_new, h_last[:, 0, :], s_last, conv_f_new)


def _run_group(x, conv_a, lru, k_cache, v_cache, hgrn, conv_f, params, final_g, depth):
    bsz, t_len, d = x.shape
    k_all = v_all = y = None
    states = []
    for l in range(depth):
        x, y, k_all, v_all, st = _layer(
            x, conv_a[l], lru[l], k_cache, v_cache, hgrn[l], conv_f[l], l, depth, k_all, v_all,
            final_g if l == depth - 1 else None, *params)
        states.append(st)
    conv_a_new, lru_new, hgrn_new, conv_f_new = (jnp.stack(z) for z in zip(*states))
    kv_shape = (depth, bsz, t_len, N_HEADS, LANES)
    return y, conv_a_new, lru_new, k_all.reshape(kv_shape), v_all.reshape(kv_shape), hgrn_new, conv_f_new


def kernel(x_prompt, x_sample, cache_conv_a, state_lru, cache_k, cache_v, state_hgrn, cache_conv_ffn, norm_mix_g, w_in, conv_a_w, conv_a_b, lru_gate_w, lru_gate_b, lru_lambda, hgrn_gamma, hgrn_norm_g, w_branch, w_out, norm_ffn_g, w_up, conv_f_w, conv_f_b, w_down, norm_final_g):
    depth = w_in.shape[0]
    d = x_prompt.shape[-1]
    d_ff = w_down.shape[1]
    assert d == N_HEADS * LANES and d_ff == 3 * d and w_in.shape[2] == (N_P_TILES + N_QKV_TILES) * d
    params = (norm_mix_g, w_in, conv_a_w, conv_a_b, lru_gate_w, lru_gate_b, lru_lambda, hgrn_gamma,
              hgrn_norm_g, w_branch, w_out, norm_ffn_g, w_up, conv_f_w, conv_f_b, w_down)
    final_g = norm_final_g[None]

    bp = x_prompt.shape[0]
    zeros = lambda *shape: [jnp.zeros(shape, F32)] * depth
    (y_prompt, p_conv_a, p_lru, p_k, p_v, p_hgrn, p_conv_ffn) = _run_group(
        x_prompt, zeros(bp, CONV_A - 1, d), zeros(bp, d), None, None,
        zeros(bp, N_HEADS, LANES, LANES), zeros(bp, CONV_F - 1, d_ff), params, final_g, depth)
    (y_sample, s_conv_a, s_lru, s_k, s_v, s_hgrn, s_conv_ffn) = _run_group(
        x_sample, cache_conv_a, state_lru, cache_k, cache_v, state_hgrn, cache_conv_ffn, params, final_g, depth)
    return (y_prompt, y_sample, p_conv_a, s_conv_a, p_lru, s_lru, p_k, s_k, p_v, s_v,
            p_hgrn, s_hgrn, p_conv_ffn, s_conv_ffn)
```

```python
import functools

import numpy as np
import jax
import jax.numpy as jnp
from jax import lax
from jax.experimental import pallas as pl
from jax.experimental.pallas import tpu as pltpu

F32 = jnp.float32
BF16 = jnp.bfloat16

EPS = 1e-6
LANES = 128
SUBLANES = 8
VMEM_LIMIT = 48 * 1024 * 1024
PROJ_VMEM_LIMIT = 58 * 1024 * 1024

N_HEADS = 8
LRU_GROUP = 64
C_LRU = 8.0
CONV_A = 4
CONV_F = 3
N_BRANCH = 3
HGRN_CHUNK = 64
HGRN_LEVELS = 6
HGRN_HEAD_GROUP = 4
ATT_BLOCK = 256
ATT_STREAMS = 8
ATT_LOG_CUTOFF = -104.0
ATT_LOG_DEAD = -1e30


def _cparams(semantics):
    return pltpu.CompilerParams(dimension_semantics=semantics, vmem_limit_bytes=VMEM_LIMIT)


def _sigmoid(x):
    return 1.0 / (1.0 + jnp.exp(-x))


def _softplus(x):
    return jnp.maximum(x, 0.0) + jnp.log1p(jnp.exp(-jnp.abs(x)))


def _rms_norm(x, g):
    ms = jnp.mean(x * x, axis=-1, keepdims=True)
    return x * lax.rsqrt(ms + EPS) * g


def _dot(a, b):
    return jnp.dot(a, b, preferred_element_type=F32)


def _dot_nt(a, b):
    return lax.dot_general(a, b, (((1,), (1,)), ((), ())), preferred_element_type=F32)


def _dot_tn(a, b):
    return lax.dot_general(a, b, (((0,), (0,)), ((), ())), preferred_element_type=F32)


def _shift_rows(x, tail, s):
    n_rows = x.shape[0]
    rows = lax.broadcasted_iota(jnp.int32, (SUBLANES, x.shape[1]), 0)
    rolled = pltpu.roll(x, s, 0)
    first = jnp.where(rows < s, pltpu.roll(tail, s, 0), rolled[0:SUBLANES])
    if n_rows == SUBLANES:
        return first
    return jnp.concatenate([first, rolled[SUBLANES:]], axis=0)


def _lru_tile(x, tail, h_prev, cw, cb, wg, gbias, lam, interleave=()):
    pending = list(interleave)

    def emit_other():
        if pending:
            pending.pop(0)()

    tt = x.shape[0]
    emit_other()
    y = cb + cw[CONV_A - 1:CONV_A] * x
    for s in range(1, CONV_A):
        y = y + cw[CONV_A - 1 - s:CONV_A - s] * _shift_rows(x, tail, s)

    gates = _dot(y.astype(BF16), wg) + gbias
    r = _sigmoid(gates[:, :LANES])
    i = _sigmoid(gates[:, LANES:])
    emit_other()
    log_a = (-C_LRU) * r * _softplus(-lam)
    a = jnp.exp(log_a)
    th = jnp.tanh(log_a)
    b = jnp.sqrt(-2.0 * th / (1.0 - th)) * (i * y)

    row = lax.broadcasted_iota(jnp.int32, (tt, LANES), 0)
    b = b + jnp.where(row == 0, a * h_prev, 0.0)
    s = 1
    while s < tt:
        if s in (SUBLANES // 2, 8 * SUBLANES):
            emit_other()
        if s < SUBLANES:
            a_sh = jnp.where(row < s, 1.0, pltpu.roll(a, s, 0))
            b_sh = jnp.where(row < s, 0.0, pltpu.roll(b, s, 0))
        else:
            a_sh = jnp.concatenate([jnp.ones((s, LANES), F32), a[:tt - s]], axis=0)
            b_sh = jnp.concatenate([jnp.zeros((s, LANES), F32), b[:tt - s]], axis=0)
        b = a * b_sh + b
        a = a * a_sh
        s *= 2
    while pending:
        emit_other()
    return b


def _lru_body(x_ref, prev_ref, h0_ref, cw_ref, cb_ref, wg_ref, gb_ref, lam_ref,
              ya_ref, convnew_ref, hlast_ref, tail_scr, h_scr, *, tt):
    t = pl.program_id(2)
    nt = pl.num_programs(2)

    @pl.when(t == 0)
    def _():
        tail_scr[...] = jnp.zeros_like(tail_scr)
        tail_scr[SUBLANES - (CONV_A - 1):SUBLANES, :] = prev_ref[0]
        h_scr[...] = h0_ref[0]

    h = _lru_tile(x_ref[0], tail_scr[...], h_scr[...], cw_ref[...], cb_ref[...], wg_ref[0], gb_ref[0],
                  lam_ref[...])
    tail_scr[...] = x_ref[0, tt - SUBLANES:tt, :]
    h_last = h[tt - 1:tt]
    ya_ref[0] = h.astype(BF16)
    h_scr[...] = h_last

    @pl.when(t == nt - 1)
    def _():
        convnew_ref[0] = x_ref[0, tt - (CONV_A - 1):tt, :]
        hlast_ref[0] = h_last


def _lru(xa3d, conv_prev, h0, conv_w, conv_b, wg, gb, lam):
    bsz, t_len, d_a = xa3d.shape
    nc = d_a // LANES
    tt = min(1024, t_len)
    body = functools.partial(_lru_body, tt=tt)
    return pl.pallas_call(
        body,
        grid=(bsz, nc, t_len // tt),
        in_specs=[
            pl.BlockSpec((1, tt, LANES), lambda b, c, t: (b, t, c)),
            pl.BlockSpec((1, CONV_A - 1, LANES), lambda b, c, t: (b, 0, c)),
            pl.BlockSpec((1, 1, LANES), lambda b, c, t: (b, 0, c)),
            pl.BlockSpec((CONV_A, LANES), lambda b, c, t: (0, c)),
            pl.BlockSpec((1, LANES), lambda b, c, t: (0, c)),
            pl.BlockSpec((1, LANES, 2 * LANES), lambda b, c, t: (c, 0, 0)),
            pl.BlockSpec((1, 1, 2 * LANES), lambda b, c, t: (c, 0, 0)),
            pl.BlockSpec((1, LANES), lambda b, c, t: (0, c)),
        ],
        out_specs=[
            pl.BlockSpec((1, tt, LANES), lambda b, c, t: (b, t, c)),
            pl.BlockSpec((1, CONV_A - 1, LANES), lambda b, c, t: (b, 0, c)),
            pl.BlockSpec((1, 1, LANES), lambda b, c, t: (b, 0, c)),
        ],
        out_shape=[
            jax.ShapeDtypeStruct((bsz, t_len, d_a), BF16),
            jax.ShapeDtypeStruct((bsz, CONV_A - 1, d_a), F32),
            jax.ShapeDtypeStruct((bsz, 1, d_a), F32),
        ],
        scratch_shapes=[pltpu.VMEM((SUBLANES, LANES), F32), pltpu.VMEM((1, LANES), F32)],
        compiler_params=_cparams(("arbitrary", "arbitrary", "arbitrary")),
        name="lru",
    )(xa3d, conv_prev, h0, conv_w, conv_b, wg, gb, lam)


N_P_TILES = 7
N_QKV_TILES = 3
N_STRIPS = 8
N_MM_PIECES = 4


def _w_in_tile(j):
    return jnp.where(j == 0, 0, jnp.where(j <= N_P_TILES, j + N_QKV_TILES, j - N_P_TILES))


def _strip(j):
    return jnp.clip(j - 1, 0, N_STRIPS - 1)


def _proj_in_body(*refs, aliased, fused, tm, nt):
    refs = list(refs)
    x_ref, g_ref, w_ref, cs_ref = refs[:4]
    pos = 6 if aliased else 4
    if fused:
        prev_ref, h0_ref, cw_ref, cb_ref, wg_ref, gb_ref, lam_ref = refs[pos:pos + 7]
        pos += 7
        (p_ref, k_ref, v_ref, qkv_ref, ya_ref, convnew_ref, hlast_ref,
         h_scr, xa_scr, tail_scr, hst_scr) = refs[pos:]
    else:
        p_ref, k_ref, v_ref, qkv_ref, xa_ref, h_scr = refs[pos:]
    i = pl.program_id(0)
    j = pl.program_id(1)

    @pl.when(j == 0)
    def _():
        h_scr[...] = _rms_norm(x_ref[...], g_ref[...]).astype(BF16)

    def lru_strip(interleave):
        c = j - 1
        t = i % nt

        @pl.when(t == 0)
        def _():
            tail_scr[c] = jnp.zeros((SUBLANES, LANES), F32)
            tail_scr[c, SUBLANES - (CONV_A - 1):SUBLANES, :] = prev_ref[0]
            hst_scr[c] = h0_ref[0]

        h = _lru_tile(xa_scr[c], tail_scr[c], hst_scr[c], cw_ref[...], cb_ref[...], wg_ref[0], gb_ref[0],
                      lam_ref[...], interleave)
        tail_scr[c] = xa_scr[c, tm - SUBLANES:tm, :]
        h_last = h[tm - 1:tm]
        hst_scr[c] = h_last
        ya_ref[...] = h.astype(BF16)

        @pl.when(t == nt - 1)
        def _():
            convnew_ref[0] = xa_scr[c, tm - (CONV_A - 1):tm, :]
            hlast_ref[0] = h_last

    def column_pieces(store):
        width = w_ref.shape[1] // N_MM_PIECES

        def piece(n):
            cols = slice(n * width, (n + 1) * width)
            return lambda: store(cols, _dot(h_scr[...], w_ref[:, cols]))

        return [piece(n) for n in range(N_MM_PIECES)]

    @pl.when(j == 0)
    def _():
        acc = _dot(h_scr[...], w_ref[...])
        if fused:
            for c in range(N_STRIPS):
                xa_scr[c] = acc[:, c * LANES:(c + 1) * LANES]
        else:
            xa_ref[...] = acc

    def store_p(cols, acc):
        p_ref[:, cols] = acc

    def store_q(cols, acc):
        qkv_ref[:, cols] = (acc * cs_ref[:, cols]).astype(BF16)

    @pl.when(jnp.logical_and(j >= 1, j <= N_P_TILES))
    def _():
        if fused:
            lru_strip(column_pieces(store_p))
        else:
            p_ref[...] = _dot(h_scr[...], w_ref[...])

    @pl.when(j == N_P_TILES + 1)
    def _():
        if fused:
            lru_strip(column_pieces(store_q))
        else:
            qkv_ref[...] = (_dot(h_scr[...], w_ref[...]) * cs_ref[...]).astype(BF16)

    for off, dst_ref in ((2, k_ref), (3, v_ref)):
        @pl.when(j == N_P_TILES + off)
        def _(dst_ref=dst_ref):
            acc = _dot(h_scr[...], w_ref[...])
            dst_ref[...] = acc
            qkv_ref[...] = (acc * cs_ref[...]).astype(BF16)


def _proj_in(x2d, g, w_bf16, col_scale, layer, depth, k_all, v_all, t_len, lru_args):
    m, d = x2d.shape
    n_tiles = w_bf16.shape[1] // d
    assert n_tiles == 1 + N_P_TILES + N_QKV_TILES and d == N_STRIPS * LANES
    tm = min(1024, m)
    fused = t_len % tm == 0
    nt = max(t_len // tm, 1)
    bsz = m // t_len
    aliased = k_all is not None
    qkv_tile = lambda j: jnp.clip(j - (N_P_TILES + 1), 0, N_QKV_TILES - 1)
    in_specs = [
        pl.BlockSpec((tm, d), lambda i, j: (i, 0)),
        pl.BlockSpec((1, d), lambda i, j: (0, 0)),
        pl.BlockSpec((d, d), lambda i, j: (0, _w_in_tile(j))),
        pl.BlockSpec((1, d), lambda i, j: (0, qkv_tile(j))),
    ]
    args = [x2d, g, w_bf16, col_scale]
    aliases = {}
    if aliased:
        in_specs += [pl.BlockSpec(memory_space=pl.ANY), pl.BlockSpec(memory_space=pl.ANY)]
        args += [k_all, v_all]
        aliases = {4: 1, 5: 2}
    kv_spec = pl.BlockSpec((None, tm, d), lambda i, j: (layer, i, 0))
    out_specs = [
        pl.BlockSpec((tm, d), lambda i, j: (i, jnp.clip(j - 1, 0, N_P_TILES - 1))),
        kv_spec, kv_spec,
        pl.BlockSpec((tm, d), lambda i, j: (i, qkv_tile(j))),
    ]
    out_shape = [
        jax.ShapeDtypeStruct((m, N_P_TILES * d), F32),
        jax.ShapeDtypeStruct((depth, m, d), F32),
        jax.ShapeDtypeStruct((depth, m, d), F32),
        jax.ShapeDtypeStruct((m, N_QKV_TILES * d), BF16),
    ]
    scratch = [pltpu.VMEM((tm, d), BF16)]
    if fused:
        last = lambda i, j: jnp.where(i % nt == nt - 1, _strip(j), 0)
        in_specs += [
            pl.BlockSpec((1, CONV_A - 1, LANES), lambda i, j: (i // nt, 0, _strip(j))),
            pl.BlockSpec((1, 1, LANES), lambda i, j: (i // nt, 0, _strip(j))),
            pl.BlockSpec((CONV_A, LANES), lambda i, j: (0, _strip(j))),
            pl.BlockSpec((1, LANES), lambda i, j: (0, _strip(j))),
            pl.BlockSpec((1, LANES, 2 * LANES), lambda i, j: (_strip(j), 0, 0)),
            pl.BlockSpec((1, 1, 2 * LANES), lambda i, j: (_strip(j), 0, 0)),
            pl.BlockSpec((1, LANES), lambda i, j: (0, _strip(j))),
        ]
        args += list(lru_args)
        out_specs += [
            pl.BlockSpec((tm, LANES), lambda i, j: (i, _strip(j))),
            pl.BlockSpec((1, CONV_A - 1, LANES), lambda i, j: (i // nt, 0, last(i, j))),
            pl.BlockSpec((1, 1, LANES), lambda i, j: (i // nt, 0, last(i, j))),
        ]
        out_shape += [
            jax.ShapeDtypeStruct((m, d), BF16),
            jax.ShapeDtypeStruct((bsz, CONV_A - 1, d), F32),
            jax.ShapeDtypeStruct((bsz, 1, d), F32),
        ]
        scratch += [pltpu.VMEM((N_STRIPS, tm, LANES), F32), pltpu.VMEM((N_STRIPS, SUBLANES, LANES), F32),
                    pltpu.VMEM((N_STRIPS, 1, LANES), F32)]
    else:
        out_specs.append(pl.BlockSpec((tm, d), lambda i, j: (i, 0)))
        out_shape.append(jax.ShapeDtypeStruct((m, d), F32))
    outs = pl.pallas_call(
        functools.partial(_proj_in_body, aliased=aliased, fused=fused, tm=tm, nt=nt),
        grid=(m // tm, n_tiles),
        in_specs=in_specs,
        out_specs=out_specs,
        out_shape=out_shape,
        scratch_shapes=scratch,
        input_output_aliases=aliases,
        compiler_params=pltpu.CompilerParams(dimension_semantics=("arbitrary", "arbitrary"),
                                             vmem_limit_bytes=PROJ_VMEM_LIMIT),
        name="proj_in",
    )(*args)
    if fused:
        return outs[0], outs[1], outs[2], outs[3], (outs[4], outs[5], outs[6]), None
    return outs[0], outs[1], outs[2], outs[3], None, outs[4]


def _suffix_sum(l, u2):
    hi = l.astype(BF16)
    lo = (l - hi.astype(F32)).astype(BF16)
    return _dot(jnp.concatenate([hi, lo], axis=1), u2)


def _attn_blocks(qs, kvs, u2, accs, carries, mask):
    zs = [_dot_nt(q, kb) for q, (kb, _) in zip(qs, kvs)]
    sps = [jnp.maximum(z, 0.0) + jnp.log(1.0 + jnp.exp(-jnp.abs(z))) for z in zs]
    ls = [-sp if mask is None else jnp.where(mask, -sp, 0.0) for sp in sps]
    sufs = [_suffix_sum(l, u2) for l in ls]
    ws = [jnp.exp(z - sp + (suf + carry)) for z, sp, suf, carry in zip(zs, sps, sufs, carries)]
    if mask is not None:
        ws = [jnp.where(mask, w, 0.0) for w in ws]
    accs = [acc + _dot(w.astype(BF16), vb) for acc, w, (_, vb) in zip(accs, ws, kvs)]
    return accs, [jnp.sum(l, axis=1, keepdims=True) for l in ls]


def _max_log_mass(states, n_left):
    m = None
    for nl, (_, carry) in zip(n_left, states):
        c = jnp.where(nl > 0, carry, ATT_LOG_DEAD)
        m = c if m is None else jnp.maximum(m, c)
    return jnp.max(m)


def _walk_back(qs, states, n_blocks, loads, u2):
    def alive(jj, sts):
        return (_max_log_mass(sts, [nb - jj for nb in n_blocks]) > ATT_LOG_CUTOFF).astype(jnp.int32)

    def cond(st):
        return st[1] > 0

    def body(st):
        jj, _, sts = st
        valids = [nb - 1 - jj >= 0 for nb in n_blocks]
        kvs = [load(jnp.maximum(nb - 1 - jj, 0)) for nb, load in zip(n_blocks, loads)]
        accs, row_sums = _attn_blocks(
            qs, kvs, u2, [acc for acc, _ in sts],
            [jnp.where(valid, carry, ATT_LOG_DEAD) for valid, (_, carry) in zip(valids, sts)], None)
        new = [(acc, jnp.where(valid, carry + row_sum, carry))
               for acc, valid, row_sum, (_, carry) in zip(accs, valids, row_sums, sts)]
        jj = jj + 1
        return jj, alive(jj, new), new

    jj0 = jnp.int32(0)
    return lax.while_loop(cond, body, (jj0, alive(jj0, states), states))[2]


def _diag_blocks(qs, kvs, u2):
    tq = qs[0].shape[0]
    row = lax.broadcasted_iota(jnp.int32, (tq, tq), 0)
    col = lax.broadcasted_iota(jnp.int32, (tq, tq), 1)
    accs, row_sums = _attn_blocks(qs, kvs, u2, [jnp.zeros((tq, LANES), F32)] * len(qs),
                                  [jnp.zeros((tq, 1), F32)] * len(qs), col < row)
    return list(zip(accs, row_sums))


def _attn_fresh_body(q_ref, k_ref, v_ref, u_ref, o_ref, *, tq, n_streams):
    i = pl.program_id(2)
    u2 = u_ref[...]

    def load(j):
        start = pl.multiple_of(j * tq, tq)
        return k_ref[0, pl.ds(start, tq), :], v_ref[0, pl.ds(start, tq), :]

    n_prev = [i * n_streams + s for s in range(n_streams)]
    qs = [q_ref[0, s * tq:(s + 1) * tq, :] for s in range(n_streams)]
    states = _diag_blocks(qs, [load(qi) for qi in n_prev], u2)
    states = _walk_back(qs, states, n_prev, [load] * n_streams, u2)
    for s in range(n_streams):
        o_ref[0, s * tq:(s + 1) * tq, :] = states[s][0].astype(BF16)


def _strict_upper2(n):
    u = np.triu(np.ones((n, n), np.float32), 1).T
    return jnp.asarray(np.concatenate([u, u], axis=0), dtype=BF16)


def _attn_fresh(qkv3d):
    bsz, t_len, d3 = qkv3d.shape
    d_b = d3 // 3
    nh = d_b // LANES
    tq = min(ATT_BLOCK, t_len)
    n_streams = min(ATT_STREAMS, t_len // tq)
    tstep = tq * n_streams
    body = functools.partial(_attn_fresh_body, tq=tq, n_streams=n_streams)
    return pl.pallas_call(
        body,
        grid=(bsz, nh, t_len // tstep),
        in_specs=[
            pl.BlockSpec((1, tstep, LANES), lambda b, h, i: (b, i, h)),
            pl.BlockSpec((1, t_len, LANES), lambda b, h, i: (b, 0, nh + h)),
            pl.BlockSpec((1, t_len, LANES), lambda b, h, i: (b, 0, 2 * nh + h)),
            pl.BlockSpec((2 * tq, tq), lambda b, h, i: (0, 0)),
        ],
        out_specs=pl.BlockSpec((1, tstep, LANES), lambda b, h, i: (b, i, h)),
        out_shape=jax.ShapeDtypeStruct((bsz, t_len, d_b), BF16),
        compiler_params=_cparams(("arbitrary", "arbitrary", "arbitrary")),
        name="attn_fresh",
    )(qkv3d, qkv3d, qkv3d, _strict_upper2(tq))


def _attn_cached_body(q_ref, k_ref, v_ref, u_ref, kp_hbm, vp_hbm, up_ref, o_ref, kbuf, vbuf, sem,
                      *, layer, tkp, n_past):
    b = pl.program_id(0)
    heads = range(N_HEADS)
    cols = [slice(h * LANES, (h + 1) * LANES) for h in heads]
    qs = [q_ref[0, :, c] for c in cols]
    states = _diag_blocks(qs, [(k_ref[0, :, c], v_ref[0, :, c]) for c in cols], u_ref[...])
    up2 = up_ref[...]
    block_rows = tkp * N_HEADS

    def block_copies(j):
        src = pl.ds(pl.multiple_of(j * block_rows, block_rows), block_rows)
        return (pltpu.make_async_copy(kp_hbm.at[layer, b, src], kbuf, sem.at[0]),
                pltpu.make_async_copy(vp_hbm.at[layer, b, src], vbuf, sem.at[1]))

    def alive(jj, sts):
        return (_max_log_mass(sts, [n_past - jj] * N_HEADS) > ATT_LOG_CUTOFF).astype(jnp.int32)

    def cond(st):
        return st[1] > 0

    def body(st):
        jj, _, sts = st
        copy_k, copy_v = block_copies(n_past - 1 - jj)
        copy_k.start()
        copy_v.start()
        copy_k.wait()
        copy_v.wait()
        kvs = [(kbuf[pl.ds(h, tkp, stride=N_HEADS), :].astype(BF16),
                vbuf[pl.ds(h, tkp, stride=N_HEADS), :].astype(BF16)) for h in heads]
        accs, row_sums = _attn_blocks(qs, kvs, up2, [acc for acc, _ in sts], [carry for _, carry in sts], None)
        new = [(acc, carry + row_sum) for acc, row_sum, (_, carry) in zip(accs, row_sums, sts)]
        jj = jj + 1
        return jj, alive(jj, new), new

    jj0 = jnp.int32(0)
    states = lax.while_loop(cond, body, (jj0, alive(jj0, states), states))[2]
    for h in heads:
        o_ref[0, :, cols[h]] = states[h][0].astype(BF16)


def _attn_cached(qkv3d, k_cache, v_cache, layer):
    bsz, t_len, d3 = qkv3d.shape
    d = d3 // 3
    depth, _, past, nh, dh = k_cache.shape
    assert nh == N_HEADS and dh == LANES and d == nh * dh and t_len <= ATT_BLOCK
    tkp = min(ATT_BLOCK, past)
    kc = k_cache.reshape(depth, bsz, past * nh, dh)
    vc = v_cache.reshape(depth, bsz, past * nh, dh)
    new_spec = lambda tile: pl.BlockSpec((1, t_len, d), lambda b: (b, 0, tile))
    body = functools.partial(_attn_cached_body, layer=layer, tkp=tkp, n_past=past // tkp)
    return pl.pallas_call(
        body,
        grid=(bsz,),
        in_specs=[
            new_spec(0), new_spec(1), new_spec(2),
            pl.BlockSpec((2 * t_len, t_len), lambda b: (0, 0)),
            pl.BlockSpec(memory_space=pl.ANY), pl.BlockSpec(memory_space=pl.ANY),
            pl.BlockSpec((2 * tkp, tkp), lambda b: (0, 0)),
        ],
        out_specs=pl.BlockSpec((1, t_len, d), lambda b: (b, 0, 0)),
        out_shape=jax.ShapeDtypeStruct((bsz, t_len, d), BF16),
        scratch_shapes=[pltpu.VMEM((tkp * nh, dh), F32), pltpu.VMEM((tkp * nh, dh), F32),
                        pltpu.SemaphoreType.DMA((2,))],
        compiler_params=_cparams(("arbitrary",)),
        name="attn_cached",
    )(qkv3d, qkv3d, qkv3d, _strict_upper2(t_len), kc, vc, _strict_upper2(tkp))


def _hgrn_tables():
    c = HGRN_CHUNK
    n_seg = 2 + HGRN_LEVELS
    d = np.zeros((n_seg * c, c), np.float32)
    jj = np.arange(c)
    amask = np.zeros((HGRN_LEVELS + 1, c, c), np.float32)
    amask[HGRN_LEVELS] = np.eye(c)
    for r in range(c):
        d[r] = jj <= r
        d[c + r] = jj > r
    for li in range(HGRN_LEVELS):
        m = c >> (li + 1)
        for r in range(c):
            blk, pos = divmod(r, 2 * m)
            boundary = blk * 2 * m + m - 1
            if pos >= m:
                d[(2 + li) * c + r] = (jj > boundary) & (jj <= r)
            else:
                d[(2 + li) * c + r] = (jj > r) & (jj <= boundary)
        late = (jj % (2 * m)) >= m
        blk_id = jj // (2 * m)
        amask[li] = (blk_id[:, None] == blk_id[None, :]) & late[:, None] & ~late[None, :]
    d3 = np.concatenate([d, d, d], axis=1)
    gmask = np.stack([np.kron(np.eye(HGRN_HEAD_GROUP, dtype=np.float32), a) for a in amask])
    return jnp.asarray(d3, dtype=BF16), jnp.asarray(gmask)


def _hgrn_chunk_heads(qr, fr, v, gr, sts, lb, log_lb, log_1mlb, ng, d3, amask_ref):
    c = HGRN_CHUNK
    q = qr * _sigmoid(qr)
    ls = jnp.minimum(fr, 0.0) - jnp.log1p(jnp.exp(-jnp.abs(fr)))
    x2 = log_1mlb + ls
    mx = jnp.maximum(log_lb, x2)
    mn = jnp.minimum(log_lb, x2)
    lf = mx + jnp.log1p(jnp.exp(mn - mx))
    kk = (1.0 - lb) * _sigmoid(-fr)

    hi = lf.astype(BF16)
    r1 = lf - hi.astype(F32)
    mid = r1.astype(BF16)
    lo = (r1 - mid.astype(F32)).astype(BF16)
    ex = jnp.exp(_dot(d3, jnp.concatenate([hi, mid, lo], axis=0)))
    xb = ex[0:c]
    qin = (q * xb).astype(BF16)
    kr = (kk * ex[c:2 * c]).astype(BF16)
    qb, kb, vb = q.astype(BF16), kk.astype(BF16), v.astype(BF16)
    qms = [(q * ex[(2 + li) * c:(3 + li) * c]).astype(BF16) for li in range(HGRN_LEVELS)]
    kms = [(kk * ex[(2 + li) * c:(3 + li) * c]).astype(BF16) for li in range(HGRN_LEVELS)]
    gate = gr * _sigmoid(gr)

    def rows(x):
        return jnp.concatenate([x[:, h * LANES:(h + 1) * LANES] for h in range(len(sts))], axis=0)

    att = amask_ref[HGRN_LEVELS] * _dot_nt(rows(qb), rows(kb))
    for li in range(HGRN_LEVELS):
        att = att + amask_ref[li] * _dot_nt(rows(qms[li]), rows(kms[li]))
    o_intra = _dot(att.astype(BF16), rows(vb))

    outs = []
    for h, st in enumerate(sts):
        cols = slice(h * LANES, (h + 1) * LANES)
        o = _dot_nt(qin[:, cols], st.astype(BF16)) + o_intra[h * c:(h + 1) * c]
        st_new = st * xb[c - 1:c, cols] + _dot_tn(vb[:, cols], kr[:, cols])
        outs.append((_rms_norm(o, ng) * gate[:, cols], st_new))
    return outs


def _hgrn_body(q_ref, f_ref, i_ref, g_ref, s0_ref, gamma_ref, ng_ref, d3_ref, amask_ref,
               y_ref, slast_ref, st_scr, *, tt, layer):
    t = pl.program_id(1)
    nt = pl.num_programs(1)
    c = HGRN_CHUNK

    @pl.when(t == 0)
    def _():
        for h in range(N_HEADS):
            st_scr[h] = s0_ref[0, h].T

    gam = gamma_ref[...]
    e = jnp.exp(gam - jnp.max(gam, axis=0, keepdims=True))
    lb_all = jnp.sum(e[1:layer + 1], axis=0, keepdims=True) / jnp.sum(e, axis=0, keepdims=True) \
        if layer > 0 else jnp.zeros((1, N_HEADS * LANES), F32)
    log_lb_all = jnp.log(lb_all)
    log_1mlb_all = jnp.log1p(-lb_all)
    ng = ng_ref[...]
    d3 = d3_ref[...]

    def chunk(ci, carry):
        r0 = pl.multiple_of(ci * c, c)
        for h0 in range(0, N_HEADS, HGRN_HEAD_GROUP):
            heads = range(h0, h0 + HGRN_HEAD_GROUP)
            cols = slice(h0 * LANES, (h0 + HGRN_HEAD_GROUP) * LANES)
            outs = _hgrn_chunk_heads(
                q_ref[0, pl.ds(r0, c), cols], f_ref[0, pl.ds(r0, c), cols],
                i_ref[0, pl.ds(r0, c), cols], g_ref[0, pl.ds(r0, c), cols],
                [st_scr[h] for h in heads], lb_all[:, cols], log_lb_all[:, cols], log_1mlb_all[:, cols],
                ng, d3, amask_ref)
            for h, (y, st_new) in zip(heads, outs):
                st_scr[h] = st_new
                y_ref[0, pl.ds(r0, c), h * LANES:(h + 1) * LANES] = y.astype(BF16)
        return carry

    lax.fori_loop(0, tt // c, chunk, 0)

    @pl.when(t == nt - 1)
    def _():
        for h in range(N_HEADS):
            slast_ref[0, h] = st_scr[h].T


def _hgrn(p3d, s0, gamma, norm_g, layer, tile0):
    bsz, t_len, _ = p3d.shape
    nh = N_HEADS
    d = nh * LANES
    depth = gamma.shape[0]
    tt = min(512, t_len)
    d3, amask = _hgrn_tables()
    body = functools.partial(_hgrn_body, tt=tt, layer=layer)

    def col_spec(off):
        return pl.BlockSpec((1, tt, d), lambda b, t: (b, t, tile0 + off))

    def const_spec(shape):
        return pl.BlockSpec(shape, lambda b, t: (0,) * len(shape))

    return pl.pallas_call(
        body,
        grid=(bsz, t_len // tt),
        in_specs=[
            col_spec(0), col_spec(1), col_spec(2), col_spec(3),
            pl.BlockSpec((1, nh, LANES, LANES), lambda b, t: (b, 0, 0, 0)),
            const_spec((depth, d)),
            const_spec((1, LANES)),
            const_spec(d3.shape), const_spec(amask.shape),
        ],
        out_specs=[
            pl.BlockSpec((1, tt, d), lambda b, t: (b, t, 0)),
            pl.BlockSpec((1, nh, LANES, LANES), lambda b, t: (b, 0, 0, 0)),
        ],
        out_shape=[
            jax.ShapeDtypeStruct((bsz, t_len, d), BF16),
            jax.ShapeDtypeStruct((bsz, nh, LANES, LANES), F32),
        ],
        scratch_shapes=[pltpu.VMEM((nh, LANES, LANES), F32)],
        compiler_params=_cparams(("arbitrary", "arbitrary")),
        name="hgrn",
    )(p3d, p3d, p3d, p3d, s0, gamma, norm_g, d3, amask)


def _merge_body(x_ref, ya_ref, yb_ref, yc_ref, ga_ref, gb_ref, gc_ref, wb_ref, wo_ref, o_ref):
    merged = _sigmoid(ga_ref[...]) * _dot(ya_ref[...], wb_ref[0])
    merged = merged + _sigmoid(gb_ref[...]) * _dot(yb_ref[...], wb_ref[1])
    merged = merged + _sigmoid(gc_ref[...]) * _dot(yc_ref[...], wb_ref[2])
    o_ref[...] = x_ref[...] + _dot(merged.astype(BF16), wo_ref[...])


def _merge(x2d, ya, yb, yc, p2d, w_branch, w_out, gtile0):
    m, d = x2d.shape
    tm = min(512, m)
    row = lambda i: (i, 0)
    return pl.pallas_call(
        _merge_body,
        grid=(m // tm,),
        in_specs=[
            pl.BlockSpec((tm, d), row),
            pl.BlockSpec((tm, d), row), pl.BlockSpec((tm, d), row), pl.BlockSpec((tm, d), row),
            pl.BlockSpec((tm, d), lambda i: (i, gtile0)),
            pl.BlockSpec((tm, d), lambda i: (i, gtile0 + 1)),
            pl.BlockSpec((tm, d), lambda i: (i, gtile0 + 2)),
            pl.BlockSpec((N_BRANCH, d, d), lambda i: (0, 0, 0), pipeline_mode=pl.Buffered(1)),
            pl.BlockSpec((d, d), lambda i: (0, 0), pipeline_mode=pl.Buffered(1)),
        ],
        out_specs=pl.BlockSpec((tm, d), row),
        out_shape=jax.ShapeDtypeStruct((m, d), F32),
        compiler_params=_cparams(("arbitrary",)),
        name="merge",
    )(x2d, ya, yb, yc, p2d, p2d, p2d, w_branch, w_out)


def _gelu_tanh(x):
    return 0.5 * x * (1.0 + jnp.tanh(np.sqrt(2.0 / np.pi).astype(np.float32) * (x + 0.044715 * (x * x * x))))


def _ffn_body(*refs, nb, tm, final):
    if final:
        (x_ref, g_ref, wu_ref, wv_ref, wd_ref, cw_ref, cb_ref, prev_ref, gf_ref,
         o_ref, convnew_ref, y_ref, h_scr, acc_scr, tail_scr) = refs
    else:
        (x_ref, g_ref, wu_ref, wv_ref, wd_ref, cw_ref, cb_ref, prev_ref,
         o_ref, convnew_ref, h_scr, acc_scr, tail_scr) = refs
    t = pl.program_id(1)
    nt = pl.num_programs(1)
    k = pl.program_id(2)
    nk = pl.num_programs(2)
    d = x_ref.shape[-1]

    @pl.when(k == 0)
    def _():
        h_scr[...] = _rms_norm(x_ref[...].reshape(nb * tm, d), g_ref[...]).astype(BF16)
        acc_scr[...] = jnp.zeros_like(acc_scr)

    @pl.when(t == 0)
    def _():
        tail_scr[k] = jnp.zeros(tail_scr.shape[1:], F32)
        tail_scr[k, :, SUBLANES - (CONV_F - 1):SUBLANES, :] = prev_ref[...]

    h = h_scr[...]
    u = _dot(h, wu_ref[...])
    v = _dot(h, wv_ref[...])
    ucs = []
    for bi in range(nb):
        ub = u[bi * tm:(bi + 1) * tm]
        tail = tail_scr[k, bi]
        uc = cb_ref[...] + cw_ref[CONV_F - 1:CONV_F, :] * ub
        for s in range(1, CONV_F):
            uc = uc + cw_ref[CONV_F - 1 - s:CONV_F - s, :] * _shift_rows(ub, tail, s)
        ucs.append(uc)
        tail_scr[k, bi] = ub[tm - SUBLANES:tm]
    uc = ucs[0] if nb == 1 else jnp.concatenate(ucs, axis=0)
    acc_scr[...] += _dot((_gelu_tanh(uc) * v).astype(BF16), wd_ref[...])

    @pl.when(t == nt - 1)
    def _():
        for bi in range(nb):
            convnew_ref[bi] = u[(bi + 1) * tm - (CONV_F - 1):(bi + 1) * tm]

    @pl.when(k == nk - 1)
    def _():
        x_new = x_ref[...].reshape(nb * tm, d) + acc_scr[...]
        o_ref[...] = x_new.reshape(nb, tm, d)
        if final:
            y_ref[...] = _rms_norm(x_new, gf_ref[...]).reshape(nb, tm, d)


def _ffn(x3d, g, w_up, w_down, conv_w, conv_b, conv_prev, final_g):
    bsz, t_len, d = x3d.shape
    d_ff = w_down.shape[0]
    tm = min(1024, t_len)
    nt = t_len // tm
    nb = min(bsz, 1024 // tm) if nt == 1 else 1
    assert bsz % nb == 0
    kc = 512
    nk = d_ff // kc
    final = final_g is not None
    body = functools.partial(_ffn_body, nb=nb, tm=tm, final=final)
    xspec = pl.BlockSpec((nb, tm, d), lambda b, t, k: (b, t, 0))
    in_specs = [
        xspec,
        pl.BlockSpec((1, d), lambda b, t, k: (0, 0)),
        pl.BlockSpec((d, kc), lambda b, t, k: (0, k)),
        pl.BlockSpec((d, kc), lambda b, t, k: (0, nk + k)),
        pl.BlockSpec((kc, d), lambda b, t, k: (k, 0)),
        pl.BlockSpec((CONV_F, kc), lambda b, t, k: (0, k)),
        pl.BlockSpec((1, kc), lambda b, t, k: (0, k)),
        pl.BlockSpec((nb, CONV_F - 1, kc), lambda b, t, k: (b, 0, k)),
    ]
    args = [x3d, g, w_up, w_up, w_down, conv_w, conv_b, conv_prev]
    convnew_spec = pl.BlockSpec((nb, CONV_F - 1, kc), lambda b, t, k: (b, 0, jnp.where(t == nt - 1, k, 0)))
    out_specs = [xspec, convnew_spec]
    out_shape = [jax.ShapeDtypeStruct((bsz, t_len, d), F32), jax.ShapeDtypeStruct((bsz, CONV_F - 1, d_ff), F32)]
    if final:
        in_specs.append(pl.BlockSpec((1, d), lambda b, t, k: (0, 0)))
        args.append(final_g)
        out_specs.append(xspec)
        out_shape.append(jax.ShapeDtypeStruct((bsz, t_len, d), F32))
    return pl.pallas_call(
        body,
        grid=(bsz // nb, nt, nk),
        in_specs=in_specs,
        out_specs=out_specs,
        out_shape=out_shape,
        scratch_shapes=[pltpu.VMEM((nb * tm, d), BF16), pltpu.VMEM((nb * tm, d), F32),
                        pltpu.VMEM((nk, nb, SUBLANES, kc), F32)],
        compiler_params=_cparams(("arbitrary", "arbitrary", "arbitrary")),
        name="ffn",
    )(*args)


def _lru_gate_weights(gate_w, gate_b):
    n_groups = gate_w.shape[1]
    per = LANES // LRU_GROUP
    nc = n_groups // per
    w = gate_w.reshape(2, nc, per, LRU_GROUP, LRU_GROUP)
    eye = jnp.eye(per, dtype=gate_w.dtype)
    wg = jnp.einsum('gcpij,pq->cpigqj', w, eye).reshape(nc, LANES, 2 * LANES)
    gb = gate_b.reshape(2, nc, LANES).transpose(1, 0, 2).reshape(nc, 1, 2 * LANES)
    return wg.astype(BF16), gb


def _layer(x3d, conv_a_prev, h0, k_cache, v_cache, s0, conv_f_prev, layer, depth, k_all, v_all, final_g,
           norm_mix_g, w_in, conv_a_w, conv_a_b, lru_gate_w, lru_gate_b, lru_lambda, hgrn_gamma,
           hgrn_norm_g, w_branch, w_out, norm_ffn_g, w_up, conv_f_w, conv_f_b, w_down):
    bsz, t_len, d = x3d.shape
    m = bsz * t_len
    x2d = x3d.reshape(m, d)
    col_scale = jnp.concatenate([jnp.full((1, d), LANES ** -0.5, F32), jnp.ones((1, 2 * d), F32)], axis=1)
    wg, gb = _lru_gate_weights(lru_gate_w[layer], lru_gate_b[layer])
    lru_args = (conv_a_prev, h0[:, None, :], conv_a_w[layer], conv_a_b[layer][None], wg, gb, lru_lambda[layer][None])
    p2d, k_all, v_all, qkv, lru_out, xa = _proj_in(x2d, norm_mix_g[layer][None], w_in[layer].astype(BF16), col_scale,
                                                   layer, depth, k_all, v_all, t_len, lru_args)
    p3d = p2d.reshape(bsz, t_len, -1)
    if lru_out is None:
        lru_out = _lru(xa.reshape(bsz, t_len, d), *lru_args)
    ya, conv_a_new, h_last = lru_out

    qkv3d = qkv.reshape(bsz, t_len, N_QKV_TILES * d)
    yb = _attn_fresh(qkv3d) if k_cache is None else _attn_cached(qkv3d, k_cache, v_cache, layer)

    yc, s_last = _hgrn(p3d, s0, hgrn_gamma, hgrn_norm_g[layer][None], layer, 0)

    x1 = _merge(x2d, ya.reshape(m, d), yb.reshape(m, d), yc.reshape(m, d), p2d,
                w_branch[layer].astype(BF16), w_out[layer].astype(BF16), N_P_TILES - N_BRANCH)

    outs = _ffn(x1.reshape(bsz, t_len, d), norm_ffn_g[layer][None], w_up[layer].astype(BF16),
                w_down[layer].astype(BF16), conv_f_w[layer], conv_f_b[layer][None], conv_f_prev, final_g)
    x_new, conv_f_new = outs[0], outs[1]
    y = outs[2] if final_g is not None else None
    return x_new, y, k_all, v_all, (conv_a_new, h_last[:, 0, :], s_last, conv_f_new)


def _run_group(x, conv_a, lru, k_cache, v_cache, hgrn, conv_f, params, final_g, depth):
    bsz, t_len, d = x.shape
    k_all = v_all = y = None
    states = []
    for l in range(depth):
        x, y, k_all, v_all, st = _layer(
            x, conv_a[l], lru[l], k_cache, v_cache, hgrn[l], conv_f[l], l, depth, k_all, v_all,
            final_g if l == depth - 1 else None, *params)
        states.append(st)
    conv_a_new, lru_new, hgrn_new, conv_f_new = (jnp.stack(z) for z in zip(*states))
    kv_shape = (depth, bsz, t_len, N_HEADS, LANES)
    return y, conv_a_new, lru_new, k_all.reshape(kv_shape), v_all.reshape(kv_shape), hgrn_new, conv_f_new


def kernel(x_prompt, x_sample, cache_conv_a, state_lru, cache_k, cache_v, state_hgrn, cache_conv_ffn, norm_mix_g, w_in, conv_a_w, conv_a_b, lru_gate_w, lru_gate_b, lru_lambda, hgrn_gamma, hgrn_norm_g, w_branch, w_out, norm_ffn_g, w_up, conv_f_w, conv_f_b, w_down, norm_final_g):
    depth = w_in.shape[0]
    d = x_prompt.shape[-1]
    d_ff = w_down.shape[1]
    assert d == N_HEADS * LANES and d_ff == 3 * d and w_in.shape[2] == (1 + N_P_TILES + N_QKV_TILES) * d
    params = (norm_mix_g, w_in, conv_a_w, conv_a_b, lru_gate_w, lru_gate_b, lru_lambda, hgrn_gamma,
              hgrn_norm_g, w_branch, w_out, norm_ffn_g, w_up, conv_f_w, conv_f_b, w_down)
    final_g = norm_final_g[None]

    bp = x_prompt.shape[0]
    zeros = lambda *shape: [jnp.zeros(shape, F32)] * depth
    (y_prompt, p_conv_a, p_lru, p_k, p_v, p_hgrn, p_conv_ffn) = _run_group(
        x_prompt, zeros(bp, CONV_A - 1, d), zeros(bp, d), None, None,
        zeros(bp, N_HEADS, LANES, LANES), zeros(bp, CONV_F - 1, d_ff), params, final_g, depth)
    (y_sample, s_conv_a, s_lru, s_k, s_v, s_hgrn, s_conv_ffn) = _run_group(
        x_sample, cache_conv_a, state_lru, cache_k, cache_v, state_hgrn, cache_conv_ffn, params, final_g, depth)
    return (y_prompt, y_sample, p_conv_a, s_conv_a, p_lru, s_lru, p_k, s_k, p_v, s_v,
            p_hgrn, s_hgrn, p_conv_ffn, s_conv_ffn)
```

```python
import functools

import numpy as np
import jax
import jax.numpy as jnp
from jax import lax
from jax.experimental import pallas as pl
from jax.experimental.pallas import tpu as pltpu

F32 = jnp.float32
BF16 = jnp.bfloat16

EPS = 1e-6
LANES = 128
SUBLANES = 8
VMEM_LIMIT = 48 * 1024 * 1024
PROJ_VMEM_LIMIT = 58 * 1024 * 1024

N_HEADS = 8
LRU_GROUP = 64
C_LRU = 8.0
CONV_A = 4
CONV_F = 3
N_BRANCH = 3
HGRN_CHUNK = 64
HGRN_LEVELS = 6
HGRN_HEAD_GROUP = 4
ATT_BLOCK = 256
ATT_STREAMS = 8
ATT_LOG_CUTOFF = -104.0
ATT_LOG_DEAD = -1e30


def _cparams(semantics):
    return pltpu.CompilerParams(dimension_semantics=semantics, vmem_limit_bytes=VMEM_LIMIT)


def _sigmoid(x):
    return 1.0 / (1.0 + jnp.exp(-x))


def _softplus(x):
    return jnp.maximum(x, 0.0) + jnp.log1p(jnp.exp(-jnp.abs(x)))


def _rms_norm(x, g):
    ms = jnp.mean(x * x, axis=-1, keepdims=True)
    return x * lax.rsqrt(ms + EPS) * g


def _dot(a, b):
    return jnp.dot(a, b, preferred_element_type=F32)


def _dot_nt(a, b):
    return lax.dot_general(a, b, (((1,), (1,)), ((), ())), preferred_element_type=F32)


def _dot_tn(a, b):
    return lax.dot_general(a, b, (((0,), (0,)), ((), ())), preferred_element_type=F32)


def _shift_rows(x, tail, s):
    n_rows = x.shape[0]
    rows = lax.broadcasted_iota(jnp.int32, (SUBLANES, x.shape[1]), 0)
    rolled = pltpu.roll(x, s, 0)
    first = jnp.where(rows < s, pltpu.roll(tail, s, 0), rolled[0:SUBLANES])
    if n_rows == SUBLANES:
        return first
    return jnp.concatenate([first, rolled[SUBLANES:]], axis=0)


def _lru_tile(x, tail, h_prev, cw, cb, wg, gbias, lam, interleave=()):
    pending = list(interleave)

    def emit_other():
        if pending:
            pending.pop(0)()

    tt = x.shape[0]
    emit_other()
    y = cb + cw[CONV_A - 1:CONV_A] * x
    for s in range(1, CONV_A):
        y = y + cw[CONV_A - 1 - s:CONV_A - s] * _shift_rows(x, tail, s)

    gates = _dot(y.astype(BF16), wg) + gbias
    r = _sigmoid(gates[:, :LANES])
    i = _sigmoid(gates[:, LANES:])
    emit_other()
    log_a = (-C_LRU) * r * _softplus(-lam)
    a = jnp.exp(log_a)
    th = jnp.tanh(log_a)
    b = jnp.sqrt(-2.0 * th / (1.0 - th)) * (i * y)

    row = lax.broadcasted_iota(jnp.int32, (tt, LANES), 0)
    b = b + jnp.where(row == 0, a * h_prev, 0.0)
    s = 1
    while s < tt:
        if s in (SUBLANES // 2, 8 * SUBLANES):
            emit_other()
        if s < SUBLANES:
            a_sh = jnp.where(row < s, 1.0, pltpu.roll(a, s, 0))
            b_sh = jnp.where(row < s, 0.0, pltpu.roll(b, s, 0))
        else:
            a_sh = jnp.concatenate([jnp.ones((s, LANES), F32), a[:tt - s]], axis=0)
            b_sh = jnp.concatenate([jnp.zeros((s, LANES), F32), b[:tt - s]], axis=0)
        b = a * b_sh + b
        a = a * a_sh
        s *= 2
    while pending:
        emit_other()
    return b


def _lru_body(x_ref, prev_ref, h0_ref, cw_ref, cb_ref, wg_ref, gb_ref, lam_ref,
              ya_ref, convnew_ref, hlast_ref, tail_scr, h_scr, *, tt):
    t = pl.program_id(2)
    nt = pl.num_programs(2)

    @pl.when(t == 0)
    def _():
        tail_scr[...] = jnp.zeros_like(tail_scr)
        tail_scr[SUBLANES - (CONV_A - 1):SUBLANES, :] = prev_ref[0]
        h_scr[...] = h0_ref[0]

    h = _lru_tile(x_ref[0], tail_scr[...], h_scr[...], cw_ref[...], cb_ref[...], wg_ref[0], gb_ref[0],
                  lam_ref[...])
    tail_scr[...] = x_ref[0, tt - SUBLANES:tt, :]
    h_last = h[tt - 1:tt]
    ya_ref[0] = h.astype(BF16)
    h_scr[...] = h_last

    @pl.when(t == nt - 1)
    def _():
        convnew_ref[0] = x_ref[0, tt - (CONV_A - 1):tt, :]
        hlast_ref[0] = h_last


def _lru(xa3d, conv_prev, h0, conv_w, conv_b, wg, gb, lam):
    bsz, t_len, d_a = xa3d.shape
    nc = d_a // LANES
    tt = min(1024, t_len)
    body = functools.partial(_lru_body, tt=tt)
    return pl.pallas_call(
        body,
        grid=(bsz, nc, t_len // tt),
        in_specs=[
            pl.BlockSpec((1, tt, LANES), lambda b, c, t: (b, t, c)),
            pl.BlockSpec((1, CONV_A - 1, LANES), lambda b, c, t: (b, 0, c)),
            pl.BlockSpec((1, 1, LANES), lambda b, c, t: (b, 0, c)),
            pl.BlockSpec((CONV_A, LANES), lambda b, c, t: (0, c)),
            pl.BlockSpec((1, LANES), lambda b, c, t: (0, c)),
            pl.BlockSpec((1, LANES, 2 * LANES), lambda b, c, t: (c, 0, 0)),
            pl.BlockSpec((1, 1, 2 * LANES), lambda b, c, t: (c, 0, 0)),
            pl.BlockSpec((1, LANES), lambda b, c, t: (0, c)),
        ],
        out_specs=[
            pl.BlockSpec((1, tt, LANES), lambda b, c, t: (b, t, c)),
            pl.BlockSpec((1, CONV_A - 1, LANES), lambda b, c, t: (b, 0, c)),
            pl.BlockSpec((1, 1, LANES), lambda b, c, t: (b, 0, c)),
        ],
        out_shape=[
            jax.ShapeDtypeStruct((bsz, t_len, d_a), BF16),
            jax.ShapeDtypeStruct((bsz, CONV_A - 1, d_a), F32),
            jax.ShapeDtypeStruct((bsz, 1, d_a), F32),
        ],
        scratch_shapes=[pltpu.VMEM((SUBLANES, LANES), F32), pltpu.VMEM((1, LANES), F32)],
        compiler_params=_cparams(("arbitrary", "arbitrary", "arbitrary")),
        name="lru",
    )(xa3d, conv_prev, h0, conv_w, conv_b, wg, gb, lam)


N_P_TILES = 7
N_QKV_TILES = 3
N_STRIPS = 8
N_MM_PIECES = 4


def _w_in_tile(j):
    return jnp.where(j == 0, 0, jnp.where(j <= N_P_TILES, j + N_QKV_TILES, j - N_P_TILES))


def _strip(j):
    return jnp.clip(j - 1, 0, N_STRIPS - 1)


def _proj_in_body(*refs, aliased, fused, tm, nt):
    refs = list(refs)
    x_ref, g_ref, w_ref, cs_ref = refs[:4]
    pos = 6 if aliased else 4
    if fused:
        prev_ref, h0_ref, cw_ref, cb_ref, wg_ref, gb_ref, lam_ref = refs[pos:pos + 7]
        pos += 7
        (p_ref, k_ref, v_ref, qkv_ref, ya_ref, convnew_ref, hlast_ref,
         h_scr, xa_scr, tail_scr, hst_scr) = refs[pos:]
    else:
        p_ref, k_ref, v_ref, qkv_ref, xa_ref, h_scr = refs[pos:]
    i = pl.program_id(0)
    j = pl.program_id(1)

    @pl.when(j == 0)
    def _():
        h_scr[...] = _rms_norm(x_ref[...], g_ref[...]).astype(BF16)

    def lru_strip(interleave):
        c = j - 1
        t = i % nt

        @pl.when(t == 0)
        def _():
            tail_scr[c] = jnp.zeros((SUBLANES, LANES), F32)
            tail_scr[c, SUBLANES - (CONV_A - 1):SUBLANES, :] = prev_ref[0]
            hst_scr[c] = h0_ref[0]

        h = _lru_tile(xa_scr[c], tail_scr[c], hst_scr[c], cw_ref[...], cb_ref[...], wg_ref[0], gb_ref[0],
                      lam_ref[...], interleave)
        tail_scr[c] = xa_scr[c, tm - SUBLANES:tm, :]
        h_last = h[tm - 1:tm]
        hst_scr[c] = h_last
        ya_ref[...] = h.astype(BF16)

        @pl.when(t == nt - 1)
        def _():
            convnew_ref[0] = xa_scr[c, tm - (CONV_A - 1):tm, :]
            hlast_ref[0] = h_last

    def column_pieces(store):
        width = w_ref.shape[1] // N_MM_PIECES

        def piece(n):
            cols = slice(n * width, (n + 1) * width)
            return lambda: store(cols, _dot(h_scr[...], w_ref[:, cols]))

        return [piece(n) for n in range(N_MM_PIECES)]

    @pl.when(j == 0)
    def _():
        acc = _dot(h_scr[...], w_ref[...])
        if fused:
            for c in range(N_STRIPS):
                xa_scr[c] = acc[:, c * LANES:(c + 1) * LANES]
        else:
            xa_ref[...] = acc

    def store_p(cols, acc):
        p_ref[:, cols] = acc

    def store_q(cols, acc):
        qkv_ref[:, cols] = (acc * cs_ref[:, cols]).astype(BF16)

    @pl.when(jnp.logical_and(j >= 1, j <= N_P_TILES))
    def _():
        if fused:
            lru_strip(column_pieces(store_p))
        else:
            p_ref[...] = _dot(h_scr[...], w_ref[...])

    @pl.when(j == N_P_TILES + 1)
    def _():
        if fused:
            lru_strip(column_pieces(store_q))
        else:
            qkv_ref[...] = (_dot(h_scr[...], w_ref[...]) * cs_ref[...]).astype(BF16)

    for off, dst_ref in ((2, k_ref), (3, v_ref)):
        @pl.when(j == N_P_TILES + off)
        def _(dst_ref=dst_ref):
            acc = _dot(h_scr[...], w_ref[...])
            dst_ref[...] = acc
            qkv_ref[...] = (acc * cs_ref[...]).astype(BF16)


def _proj_in(x2d, g, w_bf16, col_scale, layer, depth, k_all, v_all, t_len, lru_args):
    m, d = x2d.shape
    n_tiles = w_bf16.shape[2] // d
    assert n_tiles == 1 + N_P_TILES + N_QKV_TILES and d == N_STRIPS * LANES
    tm = min(1024, m)
    fused = t_len % tm == 0
    nt = max(t_len // tm, 1)
    bsz = m // t_len
    aliased = k_all is not None
    qkv_tile = lambda j: jnp.clip(j - (N_P_TILES + 1), 0, N_QKV_TILES - 1)
    in_specs = [
        pl.BlockSpec((tm, d), lambda i, j: (i, 0)),
        pl.BlockSpec((1, d), lambda i, j: (0, 0)),
        pl.BlockSpec((None, d, d), lambda i, j: (layer, 0, _w_in_tile(j))),
        pl.BlockSpec((1, d), lambda i, j: (0, qkv_tile(j))),
    ]
    args = [x2d, g, w_bf16, col_scale]
    aliases = {}
    if aliased:
        in_specs += [pl.BlockSpec(memory_space=pl.ANY), pl.BlockSpec(memory_space=pl.ANY)]
        args += [k_all, v_all]
        aliases = {4: 1, 5: 2}
    kv_spec = pl.BlockSpec((None, tm, d), lambda i, j: (layer, i, 0))
    out_specs = [
        pl.BlockSpec((tm, d), lambda i, j: (i, jnp.clip(j - 1, 0, N_P_TILES - 1))),
        kv_spec, kv_spec,
        pl.BlockSpec((tm, d), lambda i, j: (i, qkv_tile(j))),
    ]
    out_shape = [
        jax.ShapeDtypeStruct((m, N_P_TILES * d), F32),
        jax.ShapeDtypeStruct((depth, m, d), F32),
        jax.ShapeDtypeStruct((depth, m, d), F32),
        jax.ShapeDtypeStruct((m, N_QKV_TILES * d), BF16),
    ]
    scratch = [pltpu.VMEM((tm, d), BF16)]
    if fused:
        last = lambda i, j: jnp.where(i % nt == nt - 1, _strip(j), 0)
        in_specs += [
            pl.BlockSpec((1, CONV_A - 1, LANES), lambda i, j: (i // nt, 0, _strip(j))),
            pl.BlockSpec((1, 1, LANES), lambda i, j: (i // nt, 0, _strip(j))),
            pl.BlockSpec((CONV_A, LANES), lambda i, j: (0, _strip(j))),
            pl.BlockSpec((1, LANES), lambda i, j: (0, _strip(j))),
            pl.BlockSpec((1, LANES, 2 * LANES), lambda i, j: (_strip(j), 0, 0)),
            pl.BlockSpec((1, 1, 2 * LANES), lambda i, j: (_strip(j), 0, 0)),
            pl.BlockSpec((1, LANES), lambda i, j: (0, _strip(j))),
        ]
        args += list(lru_args)
        out_specs += [
            pl.BlockSpec((tm, LANES), lambda i, j: (i, _strip(j))),
            pl.BlockSpec((1, CONV_A - 1, LANES), lambda i, j: (i // nt, 0, last(i, j))),
            pl.BlockSpec((1, 1, LANES), lambda i, j: (i // nt, 0, last(i, j))),
        ]
        out_shape += [
            jax.ShapeDtypeStruct((m, d), BF16),
            jax.ShapeDtypeStruct((bsz, CONV_A - 1, d), F32),
            jax.ShapeDtypeStruct((bsz, 1, d), F32),
        ]
        scratch += [pltpu.VMEM((N_STRIPS, tm, LANES), F32), pltpu.VMEM((N_STRIPS, SUBLANES, LANES), F32),
                    pltpu.VMEM((N_STRIPS, 1, LANES), F32)]
    else:
        out_specs.append(pl.BlockSpec((tm, d), lambda i, j: (i, 0)))
        out_shape.append(jax.ShapeDtypeStruct((m, d), F32))
    outs = pl.pallas_call(
        functools.partial(_proj_in_body, aliased=aliased, fused=fused, tm=tm, nt=nt),
        grid=(m // tm, n_tiles),
        in_specs=in_specs,
        out_specs=out_specs,
        out_shape=out_shape,
        scratch_shapes=scratch,
        input_output_aliases=aliases,
        compiler_params=pltpu.CompilerParams(dimension_semantics=("arbitrary", "arbitrary"),
                                             vmem_limit_bytes=PROJ_VMEM_LIMIT),
        name="proj_in",
    )(*args)
    if fused:
        return outs[0], outs[1], outs[2], outs[3], (outs[4], outs[5], outs[6]), None
    return outs[0], outs[1], outs[2], outs[3], None, outs[4]


def _suffix_sum(l, u2):
    hi = l.astype(BF16)
    lo = (l - hi.astype(F32)).astype(BF16)
    return _dot(jnp.concatenate([hi, lo], axis=1), u2)


def _attn_blocks(qs, kvs, u2, accs, carries, mask):
    zs = [_dot_nt(q, kb) for q, (kb, _) in zip(qs, kvs)]
    sps = [jnp.maximum(z, 0.0) + jnp.log(1.0 + jnp.exp(-jnp.abs(z))) for z in zs]
    ls = [-sp if mask is None else jnp.where(mask, -sp, 0.0) for sp in sps]
    sufs = [_suffix_sum(l, u2) for l in ls]
    ws = [jnp.exp(z - sp + (suf + carry)) for z, sp, suf, carry in zip(zs, sps, sufs, carries)]
    if mask is not None:
        ws = [jnp.where(mask, w, 0.0) for w in ws]
    accs = [acc + _dot(w.astype(BF16), vb) for acc, w, (_, vb) in zip(accs, ws, kvs)]
    return accs, [jnp.sum(l, axis=1, keepdims=True) for l in ls]


def _max_log_mass(states, n_left):
    m = None
    for nl, (_, carry) in zip(n_left, states):
        c = jnp.where(nl > 0, carry, ATT_LOG_DEAD)
        m = c if m is None else jnp.maximum(m, c)
    return jnp.max(m)


def _walk_back(qs, states, n_blocks, loads, u2):
    def alive(jj, sts):
        return (_max_log_mass(sts, [nb - jj for nb in n_blocks]) > ATT_LOG_CUTOFF).astype(jnp.int32)

    def cond(st):
        return st[1] > 0

    def body(st):
        jj, _, sts = st
        valids = [nb - 1 - jj >= 0 for nb in n_blocks]
        kvs = [load(jnp.maximum(nb - 1 - jj, 0)) for nb, load in zip(n_blocks, loads)]
        accs, row_sums = _attn_blocks(
            qs, kvs, u2, [acc for acc, _ in sts],
            [jnp.where(valid, carry, ATT_LOG_DEAD) for valid, (_, carry) in zip(valids, sts)], None)
        new = [(acc, jnp.where(valid, carry + row_sum, carry))
               for acc, valid, row_sum, (_, carry) in zip(accs, valids, row_sums, sts)]
        jj = jj + 1
        return jj, alive(jj, new), new

    jj0 = jnp.int32(0)
    return lax.while_loop(cond, body, (jj0, alive(jj0, states), states))[2]


def _diag_blocks(qs, kvs, u2):
    tq = qs[0].shape[0]
    row = lax.broadcasted_iota(jnp.int32, (tq, tq), 0)
    col = lax.broadcasted_iota(jnp.int32, (tq, tq), 1)
    accs, row_sums = _attn_blocks(qs, kvs, u2, [jnp.zeros((tq, LANES), F32)] * len(qs),
                                  [jnp.zeros((tq, 1), F32)] * len(qs), col < row)
    return list(zip(accs, row_sums))


def _attn_fresh_body(q_ref, k_ref, v_ref, u_ref, o_ref, *, tq, n_streams):
    i = pl.program_id(2)
    u2 = u_ref[...]

    def load(j):
        start = pl.multiple_of(j * tq, tq)
        return k_ref[0, pl.ds(start, tq), :], v_ref[0, pl.ds(start, tq), :]

    n_prev = [i * n_streams + s for s in range(n_streams)]
    qs = [q_ref[0, s * tq:(s + 1) * tq, :] for s in range(n_streams)]
    states = _diag_blocks(qs, [load(qi) for qi in n_prev], u2)
    states = _walk_back(qs, states, n_prev, [load] * n_streams, u2)
    for s in range(n_streams):
        o_ref[0, s * tq:(s + 1) * tq, :] = states[s][0].astype(BF16)


def _strict_upper2(n):
    u = np.triu(np.ones((n, n), np.float32), 1).T
    return jnp.asarray(np.concatenate([u, u], axis=0), dtype=BF16)


def _attn_fresh(qkv3d):
    bsz, t_len, d3 = qkv3d.shape
    d_b = d3 // 3
    nh = d_b // LANES
    tq = min(ATT_BLOCK, t_len)
    n_streams = min(ATT_STREAMS, t_len // tq)
    tstep = tq * n_streams
    body = functools.partial(_attn_fresh_body, tq=tq, n_streams=n_streams)
    return pl.pallas_call(
        body,
        grid=(bsz, nh, t_len // tstep),
        in_specs=[
            pl.BlockSpec((1, tstep, LANES), lambda b, h, i: (b, i, h)),
            pl.BlockSpec((1, t_len, LANES), lambda b, h, i: (b, 0, nh + h)),
            pl.BlockSpec((1, t_len, LANES), lambda b, h, i: (b, 0, 2 * nh + h)),
            pl.BlockSpec((2 * tq, tq), lambda b, h, i: (0, 0)),
        ],
        out_specs=pl.BlockSpec((1, tstep, LANES), lambda b, h, i: (b, i, h)),
        out_shape=jax.ShapeDtypeStruct((bsz, t_len, d_b), BF16),
        compiler_params=_cparams(("arbitrary", "arbitrary", "arbitrary")),
        name="attn_fresh",
    )(qkv3d, qkv3d, qkv3d, _strict_upper2(tq))


def _attn_cached_body(q_ref, k_ref, v_ref, u_ref, kp_hbm, vp_hbm, up_ref, o_ref, kbuf, vbuf, sem,
                      *, layer, tkp, n_past):
    b = pl.program_id(0)
    heads = range(N_HEADS)
    cols = [slice(h * LANES, (h + 1) * LANES) for h in heads]
    qs = [q_ref[0, :, c] for c in cols]
    states = _diag_blocks(qs, [(k_ref[0, :, c], v_ref[0, :, c]) for c in cols], u_ref[...])
    up2 = up_ref[...]
    block_rows = tkp * N_HEADS

    def block_copies(j):
        src = pl.ds(pl.multiple_of(j * block_rows, block_rows), block_rows)
        return (pltpu.make_async_copy(kp_hbm.at[layer, b, src], kbuf, sem.at[0]),
                pltpu.make_async_copy(vp_hbm.at[layer, b, src], vbuf, sem.at[1]))

    def alive(jj, sts):
        return (_max_log_mass(sts, [n_past - jj] * N_HEADS) > ATT_LOG_CUTOFF).astype(jnp.int32)

    def cond(st):
        return st[1] > 0

    def body(st):
        jj, _, sts = st
        copy_k, copy_v = block_copies(n_past - 1 - jj)
        copy_k.start()
        copy_v.start()
        copy_k.wait()
        copy_v.wait()
        kvs = [(kbuf[pl.ds(h, tkp, stride=N_HEADS), :].astype(BF16),
                vbuf[pl.ds(h, tkp, stride=N_HEADS), :].astype(BF16)) for h in heads]
        accs, row_sums = _attn_blocks(qs, kvs, up2, [acc for acc, _ in sts], [carry for _, carry in sts], None)
        new = [(acc, carry + row_sum) for acc, row_sum, (_, carry) in zip(accs, row_sums, sts)]
        jj = jj + 1
        return jj, alive(jj, new), new

    jj0 = jnp.int32(0)
    states = lax.while_loop(cond, body, (jj0, alive(jj0, states), states))[2]
    for h in heads:
        o_ref[0, :, cols[h]] = states[h][0].astype(BF16)


def _attn_cached(qkv3d, k_cache, v_cache, layer):
    bsz, t_len, d3 = qkv3d.shape
    d = d3 // 3
    depth, _, past, nh, dh = k_cache.shape
    assert nh == N_HEADS and dh == LANES and d == nh * dh and t_len <= ATT_BLOCK
    tkp = min(ATT_BLOCK, past)
    kc = k_cache.reshape(depth, bsz, past * nh, dh)
    vc = v_cache.reshape(depth, bsz, past * nh, dh)
    new_spec = lambda tile: pl.BlockSpec((1, t_len, d), lambda b: (b, 0, tile))
    body = functools.partial(_attn_cached_body, layer=layer, tkp=tkp, n_past=past // tkp)
    return pl.pallas_call(
        body,
        grid=(bsz,),
        in_specs=[
            new_spec(0), new_spec(1), new_spec(2),
            pl.BlockSpec((2 * t_len, t_len), lambda b: (0, 0)),
            pl.BlockSpec(memory_space=pl.ANY), pl.BlockSpec(memory_space=pl.ANY),
            pl.BlockSpec((2 * tkp, tkp), lambda b: (0, 0)),
        ],
        out_specs=pl.BlockSpec((1, t_len, d), lambda b: (b, 0, 0)),
        out_shape=jax.ShapeDtypeStruct((bsz, t_len, d), BF16),
        scratch_shapes=[pltpu.VMEM((tkp * nh, dh), F32), pltpu.VMEM((tkp * nh, dh), F32),
                        pltpu.SemaphoreType.DMA((2,))],
        compiler_params=_cparams(("arbitrary",)),
        name="attn_cached",
    )(qkv3d, qkv3d, qkv3d, _strict_upper2(t_len), kc, vc, _strict_upper2(tkp))


def _hgrn_tables():
    c = HGRN_CHUNK
    n_seg = 2 + HGRN_LEVELS
    d = np.zeros((n_seg * c, c), np.float32)
    jj = np.arange(c)
    amask = np.zeros((HGRN_LEVELS + 1, c, c), np.float32)
    amask[HGRN_LEVELS] = np.eye(c)
    for r in range(c):
        d[r] = jj <= r
        d[c + r] = jj > r
    for li in range(HGRN_LEVELS):
        m = c >> (li + 1)
        for r in range(c):
            blk, pos = divmod(r, 2 * m)
            boundary = blk * 2 * m + m - 1
            if pos >= m:
                d[(2 + li) * c + r] = (jj > boundary) & (jj <= r)
            else:
                d[(2 + li) * c + r] = (jj > r) & (jj <= boundary)
        late = (jj % (2 * m)) >= m
        blk_id = jj // (2 * m)
        amask[li] = (blk_id[:, None] == blk_id[None, :]) & late[:, None] & ~late[None, :]
    d3 = np.concatenate([d, d, d], axis=1)
    gmask = np.stack([np.kron(np.eye(HGRN_HEAD_GROUP, dtype=np.float32), a) for a in amask])
    return jnp.asarray(d3, dtype=BF16), jnp.asarray(gmask)


def _hgrn_chunk_heads(qr, fr, v, gr, sts, lb, log_lb, log_1mlb, ng, d3, amask_ref):
    c = HGRN_CHUNK
    q = qr * _sigmoid(qr)
    ls = jnp.minimum(fr, 0.0) - jnp.log1p(jnp.exp(-jnp.abs(fr)))
    x2 = log_1mlb + ls
    mx = jnp.maximum(log_lb, x2)
    mn = jnp.minimum(log_lb, x2)
    lf = mx + jnp.log1p(jnp.exp(mn - mx))
    kk = (1.0 - lb) * _sigmoid(-fr)

    hi = lf.astype(BF16)
    r1 = lf - hi.astype(F32)
    mid = r1.astype(BF16)
    lo = (r1 - mid.astype(F32)).astype(BF16)
    ex = jnp.exp(_dot(d3, jnp.concatenate([hi, mid, lo], axis=0)))
    xb = ex[0:c]
    qin = (q * xb).astype(BF16)
    kr = (kk * ex[c:2 * c]).astype(BF16)
    qb, kb, vb = q.astype(BF16), kk.astype(BF16), v.astype(BF16)
    qms = [(q * ex[(2 + li) * c:(3 + li) * c]).astype(BF16) for li in range(HGRN_LEVELS)]
    kms = [(kk * ex[(2 + li) * c:(3 + li) * c]).astype(BF16) for li in range(HGRN_LEVELS)]
    gate = gr * _sigmoid(gr)

    def rows(x):
        return jnp.concatenate([x[:, h * LANES:(h + 1) * LANES] for h in range(len(sts))], axis=0)

    att = amask_ref[HGRN_LEVELS] * _dot_nt(rows(qb), rows(kb))
    for li in range(HGRN_LEVELS):
        att = att + amask_ref[li] * _dot_nt(rows(qms[li]), rows(kms[li]))
    o_intra = _dot(att.astype(BF16), rows(vb))

    outs = []
    for h, st in enumerate(sts):
        cols = slice(h * LANES, (h + 1) * LANES)
        o = _dot_nt(qin[:, cols], st.astype(BF16)) + o_intra[h * c:(h + 1) * c]
        st_new = st * xb[c - 1:c, cols] + _dot_tn(vb[:, cols], kr[:, cols])
        outs.append((_rms_norm(o, ng) * gate[:, cols], st_new))
    return outs


def _hgrn_body(q_ref, f_ref, i_ref, g_ref, s0_ref, gamma_ref, ng_ref, d3_ref, amask_ref,
               y_ref, slast_ref, st_scr, *, tt, layer):
    t = pl.program_id(1)
    nt = pl.num_programs(1)
    c = HGRN_CHUNK

    @pl.when(t == 0)
    def _():
        for h in range(N_HEADS):
            st_scr[h] = s0_ref[0, h].T

    gam = gamma_ref[...]
    e = jnp.exp(gam - jnp.max(gam, axis=0, keepdims=True))
    lb_all = jnp.sum(e[1:layer + 1], axis=0, keepdims=True) / jnp.sum(e, axis=0, keepdims=True) \
        if layer > 0 else jnp.zeros((1, N_HEADS * LANES), F32)
    log_lb_all = jnp.log(lb_all)
    log_1mlb_all = jnp.log1p(-lb_all)
    ng = ng_ref[...]
    d3 = d3_ref[...]

    def chunk(ci, carry):
        r0 = pl.multiple_of(ci * c, c)
        for h0 in range(0, N_HEADS, HGRN_HEAD_GROUP):
            heads = range(h0, h0 + HGRN_HEAD_GROUP)
            cols = slice(h0 * LANES, (h0 + HGRN_HEAD_GROUP) * LANES)
            outs = _hgrn_chunk_heads(
                q_ref[0, pl.ds(r0, c), cols], f_ref[0, pl.ds(r0, c), cols],
                i_ref[0, pl.ds(r0, c), cols], g_ref[0, pl.ds(r0, c), cols],
                [st_scr[h] for h in heads], lb_all[:, cols], log_lb_all[:, cols], log_1mlb_all[:, cols],
                ng, d3, amask_ref)
            for h, (y, st_new) in zip(heads, outs):
                st_scr[h] = st_new
                y_ref[0, pl.ds(r0, c), h * LANES:(h + 1) * LANES] = y.astype(BF16)
        return carry

    lax.fori_loop(0, tt // c, chunk, 0, unroll=2 if (tt // c) % 2 == 0 else 1)

    @pl.when(t == nt - 1)
    def _():
        for h in range(N_HEADS):
            slast_ref[0, h] = st_scr[h].T


def _hgrn(p3d, s0_all, gamma, norm_g, layer, tile0):
    bsz, t_len, _ = p3d.shape
    nh = N_HEADS
    d = nh * LANES
    depth = gamma.shape[0]
    tt = min(512, t_len)
    d3, amask = _hgrn_tables()
    body = functools.partial(_hgrn_body, tt=tt, layer=layer)

    def col_spec(off):
        return pl.BlockSpec((1, tt, d), lambda b, t: (b, t, tile0 + off))

    def const_spec(shape):
        return pl.BlockSpec(shape, lambda b, t: (0,) * len(shape))

    return pl.pallas_call(
        body,
        grid=(bsz, t_len // tt),
        in_specs=[
            col_spec(0), col_spec(1), col_spec(2), col_spec(3),
            pl.BlockSpec((None, 1, nh, LANES, LANES), lambda b, t: (layer, b, 0, 0, 0)),
            const_spec((depth, d)),
            const_spec((1, LANES)),
            const_spec(d3.shape), const_spec(amask.shape),
        ],
        out_specs=[
            pl.BlockSpec((1, tt, d), lambda b, t: (b, t, 0)),
            pl.BlockSpec((1, nh, LANES, LANES), lambda b, t: (b, 0, 0, 0)),
        ],
        out_shape=[
            jax.ShapeDtypeStruct((bsz, t_len, d), BF16),
            jax.ShapeDtypeStruct((bsz, nh, LANES, LANES), F32),
        ],
        scratch_shapes=[pltpu.VMEM((nh, LANES, LANES), F32)],
        compiler_params=_cparams(("arbitrary", "arbitrary")),
        name="hgrn",
    )(p3d, p3d, p3d, p3d, s0_all, gamma, norm_g, d3, amask)


def _merge_body(x_ref, ya_ref, yb_ref, yc_ref, ga_ref, gb_ref, gc_ref, wb_ref, wo_ref, o_ref):
    merged = _sigmoid(ga_ref[...]) * _dot(ya_ref[...], wb_ref[0])
    merged = merged + _sigmoid(gb_ref[...]) * _dot(yb_ref[...], wb_ref[1])
    merged = merged + _sigmoid(gc_ref[...]) * _dot(yc_ref[...], wb_ref[2])
    o_ref[...] = x_ref[...] + _dot(merged.astype(BF16), wo_ref[...])


def _merge(x2d, ya, yb, yc, p2d, w_branch, w_out, layer, gtile0):
    m, d = x2d.shape
    tm = min(512, m)
    row = lambda i: (i, 0)
    return pl.pallas_call(
        _merge_body,
        grid=(m // tm,),
        in_specs=[
            pl.BlockSpec((tm, d), row),
            pl.BlockSpec((tm, d), row), pl.BlockSpec((tm, d), row), pl.BlockSpec((tm, d), row),
            pl.BlockSpec((tm, d), lambda i: (i, gtile0)),
            pl.BlockSpec((tm, d), lambda i: (i, gtile0 + 1)),
            pl.BlockSpec((tm, d), lambda i: (i, gtile0 + 2)),
            pl.BlockSpec((None, N_BRANCH, d, d), lambda i: (layer, 0, 0, 0), pipeline_mode=pl.Buffered(1)),
            pl.BlockSpec((None, d, d), lambda i: (layer, 0, 0), pipeline_mode=pl.Buffered(1)),
        ],
        out_specs=pl.BlockSpec((tm, d), row),
        out_shape=jax.ShapeDtypeStruct((m, d), F32),
        compiler_params=_cparams(("arbitrary",)),
        name="merge",
    )(x2d, ya, yb, yc, p2d, p2d, p2d, w_branch, w_out)


def _gelu_tanh(x):
    return 0.5 * x * (1.0 + jnp.tanh(np.sqrt(2.0 / np.pi).astype(np.float32) * (x + 0.044715 * (x * x * x))))


def _ffn_body(*refs, nb, tm, final):
    if final:
        (x_ref, g_ref, wu_ref, wv_ref, wd_ref, cw_ref, cb_ref, prev_ref, gf_ref,
         o_ref, convnew_ref, y_ref, h_scr, acc_scr, tail_scr) = refs
    else:
        (x_ref, g_ref, wu_ref, wv_ref, wd_ref, cw_ref, cb_ref, prev_ref,
         o_ref, convnew_ref, h_scr, acc_scr, tail_scr) = refs
    t = pl.program_id(1)
    nt = pl.num_programs(1)
    k = pl.program_id(2)
    nk = pl.num_programs(2)
    d = x_ref.shape[-1]

    @pl.when(k == 0)
    def _():
        h_scr[...] = _rms_norm(x_ref[...].reshape(nb * tm, d), g_ref[...]).astype(BF16)
        acc_scr[...] = jnp.zeros_like(acc_scr)

    @pl.when(t == 0)
    def _():
        tail_scr[k] = jnp.zeros(tail_scr.shape[1:], F32)
        tail_scr[k, :, SUBLANES - (CONV_F - 1):SUBLANES, :] = prev_ref[...]

    h = h_scr[...]
    u = _dot(h, wu_ref[...])
    v = _dot(h, wv_ref[...])
    ucs = []
    for bi in range(nb):
        ub = u[bi * tm:(bi + 1) * tm]
        tail = tail_scr[k, bi]
        uc = cb_ref[...] + cw_ref[CONV_F - 1:CONV_F, :] * ub
        for s in range(1, CONV_F):
            uc = uc + cw_ref[CONV_F - 1 - s:CONV_F - s, :] * _shift_rows(ub, tail, s)
        ucs.append(uc)
        tail_scr[k, bi] = ub[tm - SUBLANES:tm]
    uc = ucs[0] if nb == 1 else jnp.concatenate(ucs, axis=0)
    acc_scr[...] += _dot((_gelu_tanh(uc) * v).astype(BF16), wd_ref[...])

    @pl.when(t == nt - 1)
    def _():
        for bi in range(nb):
            convnew_ref[bi] = u[(bi + 1) * tm - (CONV_F - 1):(bi + 1) * tm]

    @pl.when(k == nk - 1)
    def _():
        x_new = x_ref[...].reshape(nb * tm, d) + acc_scr[...]
        o_ref[...] = x_new.reshape(nb, tm, d)
        if final:
            y_ref[...] = _rms_norm(x_new, gf_ref[...]).reshape(nb, tm, d)


def _ffn(x3d, g, w_up, w_down, layer, conv_w, conv_b, conv_prev, final_g):
    bsz, t_len, d = x3d.shape
    d_ff = w_down.shape[1]
    tm = min(1024, t_len)
    nt = t_len // tm
    nb = min(bsz, 1024 // tm) if nt == 1 else 1
    assert bsz % nb == 0
    kc = 512
    nk = d_ff // kc
    final = final_g is not None
    body = functools.partial(_ffn_body, nb=nb, tm=tm, final=final)
    xspec = pl.BlockSpec((nb, tm, d), lambda b, t, k: (b, t, 0))
    in_specs = [
        xspec,
        pl.BlockSpec((1, d), lambda b, t, k: (0, 0)),
        pl.BlockSpec((None, d, kc), lambda b, t, k: (layer, 0, k)),
        pl.BlockSpec((None, d, kc), lambda b, t, k: (layer, 0, nk + k)),
        pl.BlockSpec((None, kc, d), lambda b, t, k: (layer, k, 0)),
        pl.BlockSpec((CONV_F, kc), lambda b, t, k: (0, k)),
        pl.BlockSpec((1, kc), lambda b, t, k: (0, k)),
        pl.BlockSpec((nb, CONV_F - 1, kc), lambda b, t, k: (b, 0, k)),
    ]
    args = [x3d, g, w_up, w_up, w_down, conv_w, conv_b, conv_prev]
    convnew_spec = pl.BlockSpec((nb, CONV_F - 1, kc), lambda b, t, k: (b, 0, jnp.where(t == nt - 1, k, 0)))
    out_specs = [xspec, convnew_spec]
    out_shape = [jax.ShapeDtypeStruct((bsz, t_len, d), F32), jax.ShapeDtypeStruct((bsz, CONV_F - 1, d_ff), F32)]
    if final:
        in_specs.append(pl.BlockSpec((1, d), lambda b, t, k: (0, 0)))
        args.append(final_g)
        out_specs.append(xspec)
        out_shape.append(jax.ShapeDtypeStruct((bsz, t_len, d), F32))
    return pl.pallas_call(
        body,
        grid=(bsz // nb, nt, nk),
        in_specs=in_specs,
        out_specs=out_specs,
        out_shape=out_shape,
        scratch_shapes=[pltpu.VMEM((nb * tm, d), BF16), pltpu.VMEM((nb * tm, d), F32),
                        pltpu.VMEM((nk, nb, SUBLANES, kc), F32)],
        compiler_params=_cparams(("arbitrary", "arbitrary", "arbitrary")),
        name="ffn",
    )(*args)


def _lru_gate_weights(gate_w, gate_b):
    n_groups = gate_w.shape[1]
    per = LANES // LRU_GROUP
    nc = n_groups // per
    w = gate_w.reshape(2, nc, per, LRU_GROUP, LRU_GROUP)
    eye = jnp.eye(per, dtype=gate_w.dtype)
    wg = jnp.einsum('gcpij,pq->cpigqj', w, eye).reshape(nc, LANES, 2 * LANES)
    gb = gate_b.reshape(2, nc, LANES).transpose(1, 0, 2).reshape(nc, 1, 2 * LANES)
    return wg.astype(BF16), gb


def _layer(x3d, conv_a_prev, h0, k_cache, v_cache, s0_all, conv_f_prev, layer, depth, k_all, v_all, final_g,
           norm_mix_g, w_in, conv_a_w, conv_a_b, lru_gate_w, lru_gate_b, lru_lambda, hgrn_gamma,
           hgrn_norm_g, w_branch, w_out, norm_ffn_g, w_up, conv_f_w, conv_f_b, w_down):
    bsz, t_len, d = x3d.shape
    m = bsz * t_len
    x2d = x3d.reshape(m, d)
    col_scale = jnp.concatenate([jnp.full((1, d), LANES ** -0.5, F32), jnp.ones((1, 2 * d), F32)], axis=1)
    wg, gb = _lru_gate_weights(lru_gate_w[layer], lru_gate_b[layer])
    lru_args = (conv_a_prev, h0[:, None, :], conv_a_w[layer], conv_a_b[layer][None], wg, gb, lru_lambda[layer][None])
    p2d, k_all, v_all, qkv, lru_out, xa = _proj_in(x2d, norm_mix_g[layer][None], w_in, col_scale,
                                                   layer, depth, k_all, v_all, t_len, lru_args)
    p3d = p2d.reshape(bsz, t_len, -1)
    if lru_out is None:
        lru_out = _lru(xa.reshape(bsz, t_len, d), *lru_args)
    ya, conv_a_new, h_last = lru_out

    qkv3d = qkv.reshape(bsz, t_len, N_QKV_TILES * d)
    yb = _attn_fresh(qkv3d) if k_cache is None else _attn_cached(qkv3d, k_cache, v_cache, layer)

    yc, s_last = _hgrn(p3d, s0_all, hgrn_gamma, hgrn_norm_g[layer][None], layer, 0)

    x1 = _merge(x2d, ya.reshape(m, d), yb.reshape(m, d), yc.reshape(m, d), p2d,
                w_branch, w_out, layer, N_P_TILES - N_BRANCH)

    outs = _ffn(x1.reshape(bsz, t_len, d), norm_ffn_g[layer][None], w_up, w_down, layer,
                conv_f_w[layer], conv_f_b[layer][None], conv_f_prev, final_g)
    x_new, conv_f_new = outs[0], outs[1]
    y = outs[2] if final_g is not None else None
    return x_new, y, k_all, v_all, (conv_a_new, h_last[:, 0, :], s_last, conv_f_new)


def _run_group(x, conv_a, lru, k_cache, v_cache, hgrn, conv_f, params, final_g, depth):
    bsz, t_len, d = x.shape
    k_all = v_all = y = None
    states = []
    for l in range(depth):
        x, y, k_all, v_all, st = _layer(
            x, conv_a[l], lru[l], k_cache, v_cache, hgrn, conv_f[l], l, depth, k_all, v_all,
            final_g if l == depth - 1 else None, *params)
        states.append(st)
    conv_a_new, lru_new, hgrn_new, conv_f_new = (jnp.stack(z) for z in zip(*states))
    kv_shape = (depth, bsz, t_len, N_HEADS, LANES)
    return y, conv_a_new, lru_new, k_all.reshape(kv_shape), v_all.reshape(kv_shape), hgrn_new, conv_f_new


def kernel(x_prompt, x_sample, cache_conv_a, state_lru, cache_k, cache_v, state_hgrn, cache_conv_ffn, norm_mix_g, w_in, conv_a_w, conv_a_b, lru_gate_w, lru_gate_b, lru_lambda, hgrn_gamma, hgrn_norm_g, w_branch, w_out, norm_ffn_g, w_up, conv_f_w, conv_f_b, w_down, norm_final_g):
    depth = w_in.shape[0]
    d = x_prompt.shape[-1]
    d_ff = w_down.shape[1]
    assert d == N_HEADS * LANES and d_ff == 3 * d and w_in.shape[2] == (1 + N_P_TILES + N_QKV_TILES) * d
    params = (norm_mix_g, w_in.astype(BF16), conv_a_w, conv_a_b, lru_gate_w, lru_gate_b, lru_lambda, hgrn_gamma,
              hgrn_norm_g, w_branch.astype(BF16), w_out.astype(BF16), norm_ffn_g, w_up.astype(BF16), conv_f_w,
              conv_f_b, w_down.astype(BF16))
    final_g = norm_final_g[None]

    bp = x_prompt.shape[0]
    zeros = lambda *shape: [jnp.zeros(shape, F32)] * depth
    (y_prompt, p_conv_a, p_lru, p_k, p_v, p_hgrn, p_conv_ffn) = _run_group(
        x_prompt, zeros(bp, CONV_A - 1, d), zeros(bp, d), None, None,
        jnp.zeros((depth, bp, N_HEADS, LANES, LANES), F32), zeros(bp, CONV_F - 1, d_ff), params, final_g, depth)
    (y_sample, s_conv_a, s_lru, s_k, s_v, s_hgrn, s_conv_ffn) = _run_group(
        x_sample, cache_conv_a, state_lru, cache_k, cache_v, state_hgrn, cache_conv_ffn, params, final_g, depth)
    return (y_prompt, y_sample, p_conv_a, s_conv_a, p_lru, s_lru, p_k, s_k, p_v, s_v,
            p_hgrn, s_hgrn, p_conv_ffn, s_conv_ffn)
```

```python
import functools

import numpy as np
import jax
import jax.numpy as jnp
from jax import lax
from jax.experimental import pallas as pl
from jax.experimental.pallas import tpu as pltpu

F32 = jnp.float32
BF16 = jnp.bfloat16

EPS = 1e-6
LANES = 128
SUBLANES = 8
VMEM_LIMIT = 48 * 1024 * 1024
PROJ_VMEM_LIMIT = 58 * 1024 * 1024

N_HEADS = 8
LRU_GROUP = 64
C_LRU = 8.0
CONV_A = 4
CONV_F = 3
N_BRANCH = 3
HGRN_CHUNK = 64
HGRN_LEVELS = 6
HGRN_HEAD_GROUP = 4
ATT_BLOCK = 256
ATT_STREAMS = 8
ATT_LOG_CUTOFF = -104.0
ATT_LOG_DEAD = -1e30


def _cparams(semantics):
    return pltpu.CompilerParams(dimension_semantics=semantics, vmem_limit_bytes=VMEM_LIMIT)


def _sigmoid(x):
    return 1.0 / (1.0 + jnp.exp(-x))


def _softplus(x):
    return jnp.maximum(x, 0.0) + jnp.log1p(jnp.exp(-jnp.abs(x)))


def _rms_norm(x, g):
    ms = jnp.mean(x * x, axis=-1, keepdims=True)
    return x * lax.rsqrt(ms + EPS) * g


def _dot(a, b):
    return jnp.dot(a, b, preferred_element_type=F32)


def _dot_nt(a, b):
    return lax.dot_general(a, b, (((1,), (1,)), ((), ())), preferred_element_type=F32)


def _dot_tn(a, b):
    return lax.dot_general(a, b, (((0,), (0,)), ((), ())), preferred_element_type=F32)


def _shift_rows(x, tail, s):
    n_rows = x.shape[0]
    rows = lax.broadcasted_iota(jnp.int32, (SUBLANES, x.shape[1]), 0)
    rolled = pltpu.roll(x, s, 0)
    first = jnp.where(rows < s, pltpu.roll(tail, s, 0), rolled[0:SUBLANES])
    if n_rows == SUBLANES:
        return first
    return jnp.concatenate([first, rolled[SUBLANES:]], axis=0)


def _lru_tile(x, tail, h_prev, cw, cb, wg, gbias, lam, interleave=()):
    pending = list(interleave)

    def emit_other():
        if pending:
            pending.pop(0)()

    tt = x.shape[0]
    emit_other()
    y = cb + cw[CONV_A - 1:CONV_A] * x
    for s in range(1, CONV_A):
        y = y + cw[CONV_A - 1 - s:CONV_A - s] * _shift_rows(x, tail, s)

    gates = _dot(y.astype(BF16), wg) + gbias
    r = _sigmoid(gates[:, :LANES])
    i = _sigmoid(gates[:, LANES:])
    emit_other()
    log_a = (-C_LRU) * r * _softplus(-lam)
    a = jnp.exp(log_a)
    th = jnp.tanh(log_a)
    b = jnp.sqrt(-2.0 * th / (1.0 - th)) * (i * y)

    row = lax.broadcasted_iota(jnp.int32, (tt, LANES), 0)
    b = b + jnp.where(row == 0, a * h_prev, 0.0)
    s = 1
    while s < tt:
        if s in (SUBLANES // 2, 8 * SUBLANES):
            emit_other()
        if s < SUBLANES:
            a_sh = jnp.where(row < s, 1.0, pltpu.roll(a, s, 0))
            b_sh = jnp.where(row < s, 0.0, pltpu.roll(b, s, 0))
        else:
            a_sh = jnp.concatenate([jnp.ones((s, LANES), F32), a[:tt - s]], axis=0)
            b_sh = jnp.concatenate([jnp.zeros((s, LANES), F32), b[:tt - s]], axis=0)
        b = a * b_sh + b
        a = a * a_sh
        s *= 2
    while pending:
        emit_other()
    return b


def _lru_body(x_ref, prev_ref, h0_ref, cw_ref, cb_ref, wg_ref, gb_ref, lam_ref,
              ya_ref, convnew_ref, hlast_ref, tail_scr, h_scr, *, tt):
    t = pl.program_id(2)
    nt = pl.num_programs(2)

    @pl.when(t == 0)
    def _():
        tail_scr[...] = jnp.zeros_like(tail_scr)
        tail_scr[SUBLANES - (CONV_A - 1):SUBLANES, :] = prev_ref[0]
        h_scr[...] = h0_ref[0]

    h = _lru_tile(x_ref[0], tail_scr[...], h_scr[...], cw_ref[...], cb_ref[...], wg_ref[0], gb_ref[0],
                  lam_ref[...])
    tail_scr[...] = x_ref[0, tt - SUBLANES:tt, :]
    h_last = h[tt - 1:tt]
    ya_ref[0] = h.astype(BF16)
    h_scr[...] = h_last

    @pl.when(t == nt - 1)
    def _():
        convnew_ref[0] = x_ref[0, tt - (CONV_A - 1):tt, :]
        hlast_ref[0] = h_last


def _lru(xa3d, conv_prev, h0, conv_w, conv_b, wg, gb, lam):
    bsz, t_len, d_a = xa3d.shape
    nc = d_a // LANES
    tt = min(1024, t_len)
    body = functools.partial(_lru_body, tt=tt)
    return pl.pallas_call(
        body,
        grid=(bsz, nc, t_len // tt),
        in_specs=[
            pl.BlockSpec((1, tt, LANES), lambda b, c, t: (b, t, c)),
            pl.BlockSpec((1, CONV_A - 1, LANES), lambda b, c, t: (b, 0, c)),
            pl.BlockSpec((1, 1, LANES), lambda b, c, t: (b, 0, c)),
            pl.BlockSpec((CONV_A, LANES), lambda b, c, t: (0, c)),
            pl.BlockSpec((1, LANES), lambda b, c, t: (0, c)),
            pl.BlockSpec((1, LANES, 2 * LANES), lambda b, c, t: (c, 0, 0)),
            pl.BlockSpec((1, 1, 2 * LANES), lambda b, c, t: (c, 0, 0)),
            pl.BlockSpec((1, LANES), lambda b, c, t: (0, c)),
        ],
        out_specs=[
            pl.BlockSpec((1, tt, LANES), lambda b, c, t: (b, t, c)),
            pl.BlockSpec((1, CONV_A - 1, LANES), lambda b, c, t: (b, 0, c)),
            pl.BlockSpec((1, 1, LANES), lambda b, c, t: (b, 0, c)),
        ],
        out_shape=[
            jax.ShapeDtypeStruct((bsz, t_len, d_a), BF16),
            jax.ShapeDtypeStruct((bsz, CONV_A - 1, d_a), F32),
            jax.ShapeDtypeStruct((bsz, 1, d_a), F32),
        ],
        scratch_shapes=[pltpu.VMEM((SUBLANES, LANES), F32), pltpu.VMEM((1, LANES), F32)],
        compiler_params=_cparams(("arbitrary", "arbitrary", "arbitrary")),
        name="lru",
    )(xa3d, conv_prev, h0, conv_w, conv_b, wg, gb, lam)


N_P_TILES = 7
N_QKV_TILES = 3
N_STRIPS = 8
N_MM_PIECES = 4


def _w_in_tile(j):
    return jnp.where(j == 0, 0, jnp.where(j <= N_P_TILES, j + N_QKV_TILES, j - N_P_TILES))


def _strip(j):
    return jnp.clip(j - 1, 0, N_STRIPS - 1)


def _proj_in_body(*refs, aliased, fused, tm, nt):
    refs = list(refs)
    x_ref, g_ref, w_ref, cs_ref = refs[:4]
    pos = 6 if aliased else 4
    if fused:
        prev_ref, h0_ref, cw_ref, cb_ref, wg_ref, gb_ref, lam_ref = refs[pos:pos + 7]
        pos += 7
        (p_ref, k_ref, v_ref, qkv_ref, ya_ref, convnew_ref, hlast_ref,
         h_scr, xa_scr, tail_scr, hst_scr) = refs[pos:]
    else:
        p_ref, k_ref, v_ref, qkv_ref, xa_ref, h_scr = refs[pos:]
    i = pl.program_id(0)
    j = pl.program_id(1)

    @pl.when(j == 0)
    def _():
        h_scr[...] = _rms_norm(x_ref[...], g_ref[...]).astype(BF16)

    def lru_strip(interleave):
        c = j - 1
        t = i % nt

        @pl.when(t == 0)
        def _():
            tail_scr[c] = jnp.zeros((SUBLANES, LANES), F32)
            tail_scr[c, SUBLANES - (CONV_A - 1):SUBLANES, :] = prev_ref[0]
            hst_scr[c] = h0_ref[0]

        h = _lru_tile(xa_scr[c], tail_scr[c], hst_scr[c], cw_ref[...], cb_ref[...], wg_ref[0], gb_ref[0],
                      lam_ref[...], interleave)
        tail_scr[c] = xa_scr[c, tm - SUBLANES:tm, :]
        h_last = h[tm - 1:tm]
        hst_scr[c] = h_last
        ya_ref[...] = h.astype(BF16)

        @pl.when(t == nt - 1)
        def _():
            convnew_ref[0] = xa_scr[c, tm - (CONV_A - 1):tm, :]
            hlast_ref[0] = h_last

    def column_pieces(store):
        width = w_ref.shape[1] // N_MM_PIECES

        def piece(n):
            cols = slice(n * width, (n + 1) * width)
            return lambda: store(cols, _dot(h_scr[...], w_ref[:, cols]))

        return [piece(n) for n in range(N_MM_PIECES)]

    @pl.when(j == 0)
    def _():
        acc = _dot(h_scr[...], w_ref[...])
        if fused:
            for c in range(N_STRIPS):
                xa_scr[c] = acc[:, c * LANES:(c + 1) * LANES]
        else:
            xa_ref[...] = acc

    def store_p(cols, acc):
        p_ref[:, cols] = acc

    def store_q(cols, acc):
        qkv_ref[:, cols] = (acc * cs_ref[:, cols]).astype(BF16)

    @pl.when(jnp.logical_and(j >= 1, j <= N_P_TILES))
    def _():
        if fused:
            lru_strip(column_pieces(store_p))
        else:
            p_ref[...] = _dot(h_scr[...], w_ref[...])

    @pl.when(j == N_P_TILES + 1)
    def _():
        if fused:
            lru_strip(column_pieces(store_q))
        else:
            qkv_ref[...] = (_dot(h_scr[...], w_ref[...]) * cs_ref[...]).astype(BF16)

    for off, dst_ref in ((2, k_ref), (3, v_ref)):
        @pl.when(j == N_P_TILES + off)
        def _(dst_ref=dst_ref):
            acc = _dot(h_scr[...], w_ref[...])
            dst_ref[...] = acc
            qkv_ref[...] = (acc * cs_ref[...]).astype(BF16)


def _proj_in(x2d, g, w_bf16, col_scale, layer, depth, k_all, v_all, t_len, lru_args):
    m, d = x2d.shape
    n_tiles = w_bf16.shape[2] // d
    assert n_tiles == 1 + N_P_TILES + N_QKV_TILES and d == N_STRIPS * LANES
    tm = min(1024, m)
    fused = t_len % tm == 0
    nt = max(t_len // tm, 1)
    bsz = m // t_len
    aliased = k_all is not None
    qkv_tile = lambda j: jnp.clip(j - (N_P_TILES + 1), 0, N_QKV_TILES - 1)
    in_specs = [
        pl.BlockSpec((tm, d), lambda i, j: (i, 0)),
        pl.BlockSpec((1, d), lambda i, j: (0, 0)),
        pl.BlockSpec((None, d, d), lambda i, j: (layer, 0, _w_in_tile(j))),
        pl.BlockSpec((1, d), lambda i, j: (0, qkv_tile(j))),
    ]
    args = [x2d, g, w_bf16, col_scale]
    aliases = {}
    if aliased:
        in_specs += [pl.BlockSpec(memory_space=pl.ANY), pl.BlockSpec(memory_space=pl.ANY)]
        args += [k_all, v_all]
        aliases = {4: 1, 5: 2}
    kv_spec = pl.BlockSpec((None, tm, d), lambda i, j: (layer, i, 0))
    out_specs = [
        pl.BlockSpec((tm, d), lambda i, j: (i, jnp.clip(j - 1, 0, N_P_TILES - 1))),
        kv_spec, kv_spec,
        pl.BlockSpec((tm, d), lambda i, j: (i, qkv_tile(j))),
    ]
    out_shape = [
        jax.ShapeDtypeStruct((m, N_P_TILES * d), F32),
        jax.ShapeDtypeStruct((depth, m, d), F32),
        jax.ShapeDtypeStruct((depth, m, d), F32),
        jax.ShapeDtypeStruct((m, N_QKV_TILES * d), BF16),
    ]
    scratch = [pltpu.VMEM((tm, d), BF16)]
    if fused:
        last = lambda i, j: jnp.where(i % nt == nt - 1, _strip(j), 0)
        in_specs += [
            pl.BlockSpec((1, CONV_A - 1, LANES), lambda i, j: (i // nt, 0, _strip(j))),
            pl.BlockSpec((1, 1, LANES), lambda i, j: (i // nt, 0, _strip(j))),
            pl.BlockSpec((CONV_A, LANES), lambda i, j: (0, _strip(j))),
            pl.BlockSpec((1, LANES), lambda i, j: (0, _strip(j))),
            pl.BlockSpec((1, LANES, 2 * LANES), lambda i, j: (_strip(j), 0, 0)),
            pl.BlockSpec((1, 1, 2 * LANES), lambda i, j: (_strip(j), 0, 0)),
            pl.BlockSpec((1, LANES), lambda i, j: (0, _strip(j))),
        ]
        args += list(lru_args)
        out_specs += [
            pl.BlockSpec((tm, LANES), lambda i, j: (i, _strip(j))),
            pl.BlockSpec((1, CONV_A - 1, LANES), lambda i, j: (i // nt, 0, last(i, j))),
            pl.BlockSpec((1, 1, LANES), lambda i, j: (i // nt, 0, last(i, j))),
        ]
        out_shape += [
            jax.ShapeDtypeStruct((m, d), BF16),
            jax.ShapeDtypeStruct((bsz, CONV_A - 1, d), F32),
            jax.ShapeDtypeStruct((bsz, 1, d), F32),
        ]
        scratch += [pltpu.VMEM((N_STRIPS, tm, LANES), F32), pltpu.VMEM((N_STRIPS, SUBLANES, LANES), F32),
                    pltpu.VMEM((N_STRIPS, 1, LANES), F32)]
    else:
        out_specs.append(pl.BlockSpec((tm, d), lambda i, j: (i, 0)))
        out_shape.append(jax.ShapeDtypeStruct((m, d), F32))
    outs = pl.pallas_call(
        functools.partial(_proj_in_body, aliased=aliased, fused=fused, tm=tm, nt=nt),
        grid=(m // tm, n_tiles),
        in_specs=in_specs,
        out_specs=out_specs,
        out_shape=out_shape,
        scratch_shapes=scratch,
        input_output_aliases=aliases,
        compiler_params=pltpu.CompilerParams(dimension_semantics=("arbitrary", "arbitrary"),
                                             vmem_limit_bytes=PROJ_VMEM_LIMIT),
        name="proj_in",
    )(*args)
    if fused:
        return outs[0], outs[1], outs[2], outs[3], (outs[4], outs[5], outs[6]), None
    return outs[0], outs[1], outs[2], outs[3], None, outs[4]


def _suffix_sum(l, u2):
    hi = l.astype(BF16)
    lo = (l - hi.astype(F32)).astype(BF16)
    return _dot(jnp.concatenate([hi, lo], axis=1), u2)


def _attn_blocks(qs, kvs, u2, accs, carries, mask):
    zs = [_dot_nt(q, kb) for q, (kb, _) in zip(qs, kvs)]
    sps = [jnp.maximum(z, 0.0) + jnp.log(1.0 + jnp.exp(-jnp.abs(z))) for z in zs]
    ls = [-sp if mask is None else jnp.where(mask, -sp, 0.0) for sp in sps]
    sufs = [_suffix_sum(l, u2) for l in ls]
    ws = [jnp.exp(z - sp + (suf + carry)) for z, sp, suf, carry in zip(zs, sps, sufs, carries)]
    if mask is not None:
        ws = [jnp.where(mask, w, 0.0) for w in ws]
    accs = [acc + _dot(w.astype(BF16), vb) for acc, w, (_, vb) in zip(accs, ws, kvs)]
    return accs, [jnp.sum(l, axis=1, keepdims=True) for l in ls]


def _max_log_mass(states, n_left):
    m = None
    for nl, (_, carry) in zip(n_left, states):
        c = jnp.where(nl > 0, carry, ATT_LOG_DEAD)
        m = c if m is None else jnp.maximum(m, c)
    return jnp.max(m)


def _walk_back(qs, states, n_blocks, loads, u2):
    def alive(jj, sts):
        return (_max_log_mass(sts, [nb - jj for nb in n_blocks]) > ATT_LOG_CUTOFF).astype(jnp.int32)

    def cond(st):
        return st[1] > 0

    def body(st):
        jj, _, sts = st
        valids = [nb - 1 - jj >= 0 for nb in n_blocks]
        kvs = [load(jnp.maximum(nb - 1 - jj, 0)) for nb, load in zip(n_blocks, loads)]
        accs, row_sums = _attn_blocks(
            qs, kvs, u2, [acc for acc, _ in sts],
            [jnp.where(valid, carry, ATT_LOG_DEAD) for valid, (_, carry) in zip(valids, sts)], None)
        new = [(acc, jnp.where(valid, carry + row_sum, carry))
               for acc, valid, row_sum, (_, carry) in zip(accs, valids, row_sums, sts)]
        jj = jj + 1
        return jj, alive(jj, new), new

    jj0 = jnp.int32(0)
    return lax.while_loop(cond, body, (jj0, alive(jj0, states), states))[2]


def _diag_blocks(qs, kvs, u2):
    tq = qs[0].shape[0]
    row = lax.broadcasted_iota(jnp.int32, (tq, tq), 0)
    col = lax.broadcasted_iota(jnp.int32, (tq, tq), 1)
    accs, row_sums = _attn_blocks(qs, kvs, u2, [jnp.zeros((tq, LANES), F32)] * len(qs),
                                  [jnp.zeros((tq, 1), F32)] * len(qs), col < row)
    return list(zip(accs, row_sums))


def _attn_fresh_body(q_ref, k_ref, v_ref, u_ref, o_ref, *, tq, n_streams):
    i = pl.program_id(2)
    u2 = u_ref[...]

    def load(j):
        start = pl.multiple_of(j * tq, tq)
        return k_ref[0, pl.ds(start, tq), :], v_ref[0, pl.ds(start, tq), :]

    n_prev = [i * n_streams + s for s in range(n_streams)]
    qs = [q_ref[0, s * tq:(s + 1) * tq, :] for s in range(n_streams)]
    states = _diag_blocks(qs, [load(qi) for qi in n_prev], u2)
    states = _walk_back(qs, states, n_prev, [load] * n_streams, u2)
    for s in range(n_streams):
        o_ref[0, s * tq:(s + 1) * tq, :] = states[s][0].astype(BF16)


def _strict_upper2(n):
    u = np.triu(np.ones((n, n), np.float32), 1).T
    return jnp.asarray(np.concatenate([u, u], axis=0), dtype=BF16)


def _attn_fresh(qkv3d):
    bsz, t_len, d3 = qkv3d.shape
    d_b = d3 // 3
    nh = d_b // LANES
    tq = min(ATT_BLOCK, t_len)
    n_streams = min(ATT_STREAMS, t_len // tq)
    tstep = tq * n_streams
    body = functools.partial(_attn_fresh_body, tq=tq, n_streams=n_streams)
    return pl.pallas_call(
        body,
        grid=(bsz, nh, t_len // tstep),
        in_specs=[
            pl.BlockSpec((1, tstep, LANES), lambda b, h, i: (b, i, h)),
            pl.BlockSpec((1, t_len, LANES), lambda b, h, i: (b, 0, nh + h)),
            pl.BlockSpec((1, t_len, LANES), lambda b, h, i: (b, 0, 2 * nh + h)),
            pl.BlockSpec((2 * tq, tq), lambda b, h, i: (0, 0)),
        ],
        out_specs=pl.BlockSpec((1, tstep, LANES), lambda b, h, i: (b, i, h)),
        out_shape=jax.ShapeDtypeStruct((bsz, t_len, d_b), BF16),
        compiler_params=_cparams(("arbitrary", "arbitrary", "arbitrary")),
        name="attn_fresh",
    )(qkv3d, qkv3d, qkv3d, _strict_upper2(tq))


def _attn_cached_body(q_ref, k_ref, v_ref, u_ref, kp_hbm, vp_hbm, up_ref, o_ref, kbuf, vbuf, sem,
                      *, layer, tkp, n_past):
    b = pl.program_id(0)
    heads = range(N_HEADS)
    cols = [slice(h * LANES, (h + 1) * LANES) for h in heads]
    qs = [q_ref[0, :, c] for c in cols]
    states = _diag_blocks(qs, [(k_ref[0, :, c], v_ref[0, :, c]) for c in cols], u_ref[...])
    up2 = up_ref[...]
    block_rows = tkp * N_HEADS

    def block_copies(j):
        src = pl.ds(pl.multiple_of(j * block_rows, block_rows), block_rows)
        return (pltpu.make_async_copy(kp_hbm.at[layer, b, src], kbuf, sem.at[0]),
                pltpu.make_async_copy(vp_hbm.at[layer, b, src], vbuf, sem.at[1]))

    def alive(jj, sts):
        return (_max_log_mass(sts, [n_past - jj] * N_HEADS) > ATT_LOG_CUTOFF).astype(jnp.int32)

    def cond(st):
        return st[1] > 0

    def body(st):
        jj, _, sts = st
        copy_k, copy_v = block_copies(n_past - 1 - jj)
        copy_k.start()
        copy_v.start()
        copy_k.wait()
        copy_v.wait()
        kvs = [(kbuf[pl.ds(h, tkp, stride=N_HEADS), :].astype(BF16),
                vbuf[pl.ds(h, tkp, stride=N_HEADS), :].astype(BF16)) for h in heads]
        accs, row_sums = _attn_blocks(qs, kvs, up2, [acc for acc, _ in sts], [carry for _, carry in sts], None)
        new = [(acc, carry + row_sum) for acc, row_sum, (_, carry) in zip(accs, row_sums, sts)]
        jj = jj + 1
        return jj, alive(jj, new), new

    jj0 = jnp.int32(0)
    states = lax.while_loop(cond, body, (jj0, alive(jj0, states), states))[2]
    for h in heads:
        o_ref[0, :, cols[h]] = states[h][0].astype(BF16)


def _attn_cached(qkv3d, k_cache, v_cache, layer):
    bsz, t_len, d3 = qkv3d.shape
    d = d3 // 3
    depth, _, past, nh, dh = k_cache.shape
    assert nh == N_HEADS and dh == LANES and d == nh * dh and t_len <= ATT_BLOCK
    tkp = min(ATT_BLOCK, past)
    kc = k_cache.reshape(depth, bsz, past * nh, dh)
    vc = v_cache.reshape(depth, bsz, past * nh, dh)
    new_spec = lambda tile: pl.BlockSpec((1, t_len, d), lambda b: (b, 0, tile))
    body = functools.partial(_attn_cached_body, layer=layer, tkp=tkp, n_past=past // tkp)
    return pl.pallas_call(
        body,
        grid=(bsz,),
        in_specs=[
            new_spec(0), new_spec(1), new_spec(2),
            pl.BlockSpec((2 * t_len, t_len), lambda b: (0, 0)),
            pl.BlockSpec(memory_space=pl.ANY), pl.BlockSpec(memory_space=pl.ANY),
            pl.BlockSpec((2 * tkp, tkp), lambda b: (0, 0)),
        ],
        out_specs=pl.BlockSpec((1, t_len, d), lambda b: (b, 0, 0)),
        out_shape=jax.ShapeDtypeStruct((bsz, t_len, d), BF16),
        scratch_shapes=[pltpu.VMEM((tkp * nh, dh), F32), pltpu.VMEM((tkp * nh, dh), F32),
                        pltpu.SemaphoreType.DMA((2,))],
        compiler_params=_cparams(("arbitrary",)),
        name="attn_cached",
    )(qkv3d, qkv3d, qkv3d, _strict_upper2(t_len), kc, vc, _strict_upper2(tkp))


def _hgrn_tables():
    c = HGRN_CHUNK
    n_seg = 2 + HGRN_LEVELS
    d = np.zeros((n_seg * c, c), np.float32)
    jj = np.arange(c)
    amask = np.zeros((HGRN_LEVELS + 1, c, c), np.float32)
    amask[HGRN_LEVELS] = np.eye(c)
    for r in range(c):
        d[r] = jj <= r
        d[c + r] = jj > r
    for li in range(HGRN_LEVELS):
        m = c >> (li + 1)
        for r in range(c):
            blk, pos = divmod(r, 2 * m)
            boundary = blk * 2 * m + m - 1
            if pos >= m:
                d[(2 + li) * c + r] = (jj > boundary) & (jj <= r)
            else:
                d[(2 + li) * c + r] = (jj > r) & (jj <= boundary)
        late = (jj % (2 * m)) >= m
        blk_id = jj // (2 * m)
        amask[li] = (blk_id[:, None] == blk_id[None, :]) & late[:, None] & ~late[None, :]
    d3 = np.concatenate([d, d, d], axis=1)
    gmask = np.stack([np.kron(np.eye(HGRN_HEAD_GROUP, dtype=np.float32), a) for a in amask])
    return jnp.asarray(d3, dtype=BF16), jnp.asarray(gmask)


def _hgrn_chunk_heads(qr, fr, v, gr, sts, lb, lb_is_zero, ng, d3, amask_ref):
    c = HGRN_CHUNK
    q = qr * _sigmoid(qr)
    e = jnp.exp(-jnp.abs(fr))
    r = 1.0 / (1.0 + e)
    pos = fr >= 0.0
    kk = (1.0 - lb) * jnp.where(pos, e * r, r)
    if lb_is_zero:
        lf = jnp.minimum(fr, 0.0) - jnp.log(1.0 + e)
    else:
        lf = jnp.log(lb + (1.0 - lb) * jnp.where(pos, r, e * r))

    hi = lf.astype(BF16)
    r1 = lf - hi.astype(F32)
    mid = r1.astype(BF16)
    lo = (r1 - mid.astype(F32)).astype(BF16)
    ex = jnp.exp(_dot(d3, jnp.concatenate([hi, mid, lo], axis=0)))
    xb = ex[0:c]
    qin = (q * xb).astype(BF16)
    kr = (kk * ex[c:2 * c]).astype(BF16)
    qb, kb, vb = q.astype(BF16), kk.astype(BF16), v.astype(BF16)
    qms = [(q * ex[(2 + li) * c:(3 + li) * c]).astype(BF16) for li in range(HGRN_LEVELS)]
    kms = [(kk * ex[(2 + li) * c:(3 + li) * c]).astype(BF16) for li in range(HGRN_LEVELS)]
    gate = gr * _sigmoid(gr)

    def rows(x):
        return jnp.concatenate([x[:, h * LANES:(h + 1) * LANES] for h in range(len(sts))], axis=0)

    att = amask_ref[HGRN_LEVELS] * _dot_nt(rows(qb), rows(kb))
    for li in range(HGRN_LEVELS):
        att = att + amask_ref[li] * _dot_nt(rows(qms[li]), rows(kms[li]))
    o_intra = _dot(att.astype(BF16), rows(vb))

    outs = []
    for h, st in enumerate(sts):
        cols = slice(h * LANES, (h + 1) * LANES)
        o = _dot_nt(qin[:, cols], st.astype(BF16)) + o_intra[h * c:(h + 1) * c]
        st_new = st * xb[c - 1:c, cols] + _dot_tn(vb[:, cols], kr[:, cols])
        outs.append((_rms_norm(o, ng) * gate[:, cols], st_new))
    return outs


def _hgrn_body(q_ref, f_ref, i_ref, g_ref, s0_ref, gamma_ref, ng_ref, d3_ref, amask_ref,
               y_ref, slast_ref, st_scr, *, tt, layer):
    t = pl.program_id(1)
    nt = pl.num_programs(1)
    c = HGRN_CHUNK

    @pl.when(t == 0)
    def _():
        for h in range(N_HEADS):
            st_scr[h] = s0_ref[0, h].T

    gam = gamma_ref[...]
    e = jnp.exp(gam - jnp.max(gam, axis=0, keepdims=True))
    lb_all = jnp.sum(e[1:layer + 1], axis=0, keepdims=True) / jnp.sum(e, axis=0, keepdims=True) \
        if layer > 0 else jnp.zeros((1, N_HEADS * LANES), F32)
    ng = ng_ref[...]
    d3 = d3_ref[...]

    def chunk(ci, carry):
        r0 = pl.multiple_of(ci * c, c)
        for h0 in range(0, N_HEADS, HGRN_HEAD_GROUP):
            heads = range(h0, h0 + HGRN_HEAD_GROUP)
            cols = slice(h0 * LANES, (h0 + HGRN_HEAD_GROUP) * LANES)
            outs = _hgrn_chunk_heads(
                q_ref[0, pl.ds(r0, c), cols], f_ref[0, pl.ds(r0, c), cols],
                i_ref[0, pl.ds(r0, c), cols], g_ref[0, pl.ds(r0, c), cols],
                [st_scr[h] for h in heads], lb_all[:, cols], layer == 0, ng, d3, amask_ref)
            for h, (y, st_new) in zip(heads, outs):
                st_scr[h] = st_new
                y_ref[0, pl.ds(r0, c), h * LANES:(h + 1) * LANES] = y.astype(BF16)
        return carry

    lax.fori_loop(0, tt // c, chunk, 0, unroll=True)

    @pl.when(t == nt - 1)
    def _():
        for h in range(N_HEADS):
            slast_ref[0, h] = st_scr[h].T


def _hgrn(p3d, s0_all, gamma, norm_g, layer, tile0):
    bsz, t_len, _ = p3d.shape
    nh = N_HEADS
    d = nh * LANES
    depth = gamma.shape[0]
    tt = min(512, t_len)
    d3, amask = _hgrn_tables()
    body = functools.partial(_hgrn_body, tt=tt, layer=layer)

    def col_spec(off):
        return pl.BlockSpec((1, tt, d), lambda b, t: (b, t, tile0 + off))

    def const_spec(shape):
        return pl.BlockSpec(shape, lambda b, t: (0,) * len(shape))

    return pl.pallas_call(
        body,
        grid=(bsz, t_len // tt),
        in_specs=[
            col_spec(0), col_spec(1), col_spec(2), col_spec(3),
            pl.BlockSpec((None, 1, nh, LANES, LANES), lambda b, t: (layer, b, 0, 0, 0)),
            const_spec((depth, d)),
            const_spec((1, LANES)),
            const_spec(d3.shape), const_spec(amask.shape),
        ],
        out_specs=[
            pl.BlockSpec((1, tt, d), lambda b, t: (b, t, 0)),
            pl.BlockSpec((1, nh, LANES, LANES), lambda b, t: (b, 0, 0, 0)),
        ],
        out_shape=[
            jax.ShapeDtypeStruct((bsz, t_len, d), BF16),
            jax.ShapeDtypeStruct((bsz, nh, LANES, LANES), F32),
        ],
        scratch_shapes=[pltpu.VMEM((nh, LANES, LANES), F32)],
        compiler_params=_cparams(("arbitrary", "arbitrary")),
        name="hgrn",
    )(p3d, p3d, p3d, p3d, s0_all, gamma, norm_g, d3, amask)


def _merge_body(x_ref, ya_ref, yb_ref, yc_ref, ga_ref, gb_ref, gc_ref, wb_ref, wo_ref, o_ref):
    merged = _sigmoid(ga_ref[...]) * _dot(ya_ref[...], wb_ref[0])
    merged = merged + _sigmoid(gb_ref[...]) * _dot(yb_ref[...], wb_ref[1])
    merged = merged + _sigmoid(gc_ref[...]) * _dot(yc_ref[...], wb_ref[2])
    o_ref[...] = x_ref[...] + _dot(merged.astype(BF16), wo_ref[...])


def _merge(x2d, ya, yb, yc, p2d, w_branch, w_out, layer, gtile0):
    m, d = x2d.shape
    tm = min(512, m)
    row = lambda i: (i, 0)
    return pl.pallas_call(
        _merge_body,
        grid=(m // tm,),
        in_specs=[
            pl.BlockSpec((tm, d), row),
            pl.BlockSpec((tm, d), row), pl.BlockSpec((tm, d), row), pl.BlockSpec((tm, d), row),
            pl.BlockSpec((tm, d), lambda i: (i, gtile0)),
            pl.BlockSpec((tm, d), lambda i: (i, gtile0 + 1)),
            pl.BlockSpec((tm, d), lambda i: (i, gtile0 + 2)),
            pl.BlockSpec((None, N_BRANCH, d, d), lambda i: (layer, 0, 0, 0), pipeline_mode=pl.Buffered(1)),
            pl.BlockSpec((None, d, d), lambda i: (layer, 0, 0), pipeline_mode=pl.Buffered(1)),
        ],
        out_specs=pl.BlockSpec((tm, d), row),
        out_shape=jax.ShapeDtypeStruct((m, d), F32),
        compiler_params=_cparams(("arbitrary",)),
        name="merge",
    )(x2d, ya, yb, yc, p2d, p2d, p2d, w_branch, w_out)


def _gelu_tanh(x):
    return 0.5 * x * (1.0 + jnp.tanh(np.sqrt(2.0 / np.pi).astype(np.float32) * (x + 0.044715 * (x * x * x))))


def _ffn_body(*refs, nb, tm, final):
    if final:
        (x_ref, g_ref, wu_ref, wv_ref, wd_ref, cw_ref, cb_ref, prev_ref, gf_ref,
         o_ref, convnew_ref, y_ref, h_scr, acc_scr, tail_scr) = refs
    else:
        (x_ref, g_ref, wu_ref, wv_ref, wd_ref, cw_ref, cb_ref, prev_ref,
         o_ref, convnew_ref, h_scr, acc_scr, tail_scr) = refs
    t = pl.program_id(1)
    nt = pl.num_programs(1)
    k = pl.program_id(2)
    nk = pl.num_programs(2)
    d = x_ref.shape[-1]

    @pl.when(k == 0)
    def _():
        h_scr[...] = _rms_norm(x_ref[...].reshape(nb * tm, d), g_ref[...]).astype(BF16)
        acc_scr[...] = jnp.zeros_like(acc_scr)

    @pl.when(t == 0)
    def _():
        tail_scr[k] = jnp.zeros(tail_scr.shape[1:], F32)
        tail_scr[k, :, SUBLANES - (CONV_F - 1):SUBLANES, :] = prev_ref[...]

    h = h_scr[...]
    u = _dot(h, wu_ref[...])
    v = _dot(h, wv_ref[...])
    ucs = []
    for bi in range(nb):
        ub = u[bi * tm:(bi + 1) * tm]
        tail = tail_scr[k, bi]
        uc = cb_ref[...] + cw_ref[CONV_F - 1:CONV_F, :] * ub
        for s in range(1, CONV_F):
            uc = uc + cw_ref[CONV_F - 1 - s:CONV_F - s, :] * _shift_rows(ub, tail, s)
        ucs.append(uc)
        tail_scr[k, bi] = ub[tm - SUBLANES:tm]
    uc = ucs[0] if nb == 1 else jnp.concatenate(ucs, axis=0)
    acc_scr[...] += _dot((_gelu_tanh(uc) * v).astype(BF16), wd_ref[...])

    @pl.when(t == nt - 1)
    def _():
        for bi in range(nb):
            convnew_ref[bi] = u[(bi + 1) * tm - (CONV_F - 1):(bi + 1) * tm]

    @pl.when(k == nk - 1)
    def _():
        x_new = x_ref[...].reshape(nb * tm, d) + acc_scr[...]
        o_ref[...] = x_new.reshape(nb, tm, d)
        if final:
            y_ref[...] = _rms_norm(x_new, gf_ref[...]).reshape(nb, tm, d)


def _ffn(x3d, g, w_up, w_down, layer, conv_w, conv_b, conv_prev, final_g):
    bsz, t_len, d = x3d.shape
    d_ff = w_down.shape[1]
    tm = min(1024, t_len)
    nt = t_len // tm
    nb = min(bsz, 1024 // tm) if nt == 1 else 1
    assert bsz % nb == 0
    kc = 512
    nk = d_ff // kc
    final = final_g is not None
    body = functools.partial(_ffn_body, nb=nb, tm=tm, final=final)
    xspec = pl.BlockSpec((nb, tm, d), lambda b, t, k: (b, t, 0))
    in_specs = [
        xspec,
        pl.BlockSpec((1, d), lambda b, t, k: (0, 0)),
        pl.BlockSpec((None, d, kc), lambda b, t, k: (layer, 0, k)),
        pl.BlockSpec((None, d, kc), lambda b, t, k: (layer, 0, nk + k)),
        pl.BlockSpec((None, kc, d), lambda b, t, k: (layer, k, 0)),
        pl.BlockSpec((CONV_F, kc), lambda b, t, k: (0, k)),
        pl.BlockSpec((1, kc), lambda b, t, k: (0, k)),
        pl.BlockSpec((nb, CONV_F - 1, kc), lambda b, t, k: (b, 0, k)),
    ]
    args = [x3d, g, w_up, w_up, w_down, conv_w, conv_b, conv_prev]
    convnew_spec = pl.BlockSpec((nb, CONV_F - 1, kc), lambda b, t, k: (b, 0, jnp.where(t == nt - 1, k, 0)))
    out_specs = [xspec, convnew_spec]
    out_shape = [jax.ShapeDtypeStruct((bsz, t_len, d), F32), jax.ShapeDtypeStruct((bsz, CONV_F - 1, d_ff), F32)]
    if final:
        in_specs.append(pl.BlockSpec((1, d), lambda b, t, k: (0, 0)))
        args.append(final_g)
        out_specs.append(xspec)
        out_shape.append(jax.ShapeDtypeStruct((bsz, t_len, d), F32))
    return pl.pallas_call(
        body,
        grid=(bsz // nb, nt, nk),
        in_specs=in_specs,
        out_specs=out_specs,
        out_shape=out_shape,
        scratch_shapes=[pltpu.VMEM((nb * tm, d), BF16), pltpu.VMEM((nb * tm, d), F32),
                        pltpu.VMEM((nk, nb, SUBLANES, kc), F32)],
        compiler_params=_cparams(("arbitrary", "arbitrary", "arbitrary")),
        name="ffn",
    )(*args)


def _lru_gate_weights(gate_w, gate_b):
    n_groups = gate_w.shape[1]
    per = LANES // LRU_GROUP
    nc = n_groups // per
    w = gate_w.reshape(2, nc, per, LRU_GROUP, LRU_GROUP)
    eye = jnp.eye(per, dtype=gate_w.dtype)
    wg = jnp.einsum('gcpij,pq->cpigqj', w, eye).reshape(nc, LANES, 2 * LANES)
    gb = gate_b.reshape(2, nc, LANES).transpose(1, 0, 2).reshape(nc, 1, 2 * LANES)
    return wg.astype(BF16), gb


def _layer(x3d, conv_a_prev, h0, k_cache, v_cache, s0_all, conv_f_prev, layer, depth, k_all, v_all, final_g,
           norm_mix_g, w_in, conv_a_w, conv_a_b, lru_gate_w, lru_gate_b, lru_lambda, hgrn_gamma,
           hgrn_norm_g, w_branch, w_out, norm_ffn_g, w_up, conv_f_w, conv_f_b, w_down):
    bsz, t_len, d = x3d.shape
    m = bsz * t_len
    x2d = x3d.reshape(m, d)
    col_scale = jnp.concatenate([jnp.full((1, d), LANES ** -0.5, F32), jnp.ones((1, 2 * d), F32)], axis=1)
    wg, gb = _lru_gate_weights(lru_gate_w[layer], lru_gate_b[layer])
    lru_args = (conv_a_prev, h0[:, None, :], conv_a_w[layer], conv_a_b[layer][None], wg, gb, lru_lambda[layer][None])
    p2d, k_all, v_all, qkv, lru_out, xa = _proj_in(x2d, norm_mix_g[layer][None], w_in, col_scale,
                                                   layer, depth, k_all, v_all, t_len, lru_args)
    p3d = p2d.reshape(bsz, t_len, -1)
    if lru_out is None:
        lru_out = _lru(xa.reshape(bsz, t_len, d), *lru_args)
    ya, conv_a_new, h_last = lru_out

    qkv3d = qkv.reshape(bsz, t_len, N_QKV_TILES * d)
    yb = _attn_fresh(qkv3d) if k_cache is None else _attn_cached(qkv3d, k_cache, v_cache, layer)

    yc, s_last = _hgrn(p3d, s0_all, hgrn_gamma, hgrn_norm_g[layer][None], layer, 0)

    x1 = _merge(x2d, ya.reshape(m, d), yb.reshape(m, d), yc.reshape(m, d), p2d,
                w_branch, w_out, layer, N_P_TILES - N_BRANCH)

    outs = _ffn(x1.reshape(bsz, t_len, d), norm_ffn_g[layer][None], w_up, w_down, layer,
                conv_f_w[layer], conv_f_b[layer][None], conv_f_prev, final_g)
    x_new, conv_f_new = outs[0], outs[1]
    y = outs[2] if final_g is not None else None
    return x_new, y, k_all, v_all, (conv_a_new, h_last[:, 0, :], s_last, conv_f_new)


def _run_group(x, conv_a, lru, k_cache, v_cache, hgrn, conv_f, params, final_g, depth):
    bsz, t_len, d = x.shape
    k_all = v_all = y = None
    states = []
    for l in range(depth):
        x, y, k_all, v_all, st = _layer(
            x, conv_a[l], lru[l], k_cache, v_cache, hgrn, conv_f[l], l, depth, k_all, v_all,
            final_g if l == depth - 1 else None, *params)
        states.append(st)
    conv_a_new, lru_new, hgrn_new, conv_f_new = (jnp.stack(z) for z in zip(*states))
    kv_shape = (depth, bsz, t_len, N_HEADS, LANES)
    return y, conv_a_new, lru_new, k_all.reshape(kv_shape), v_all.reshape(kv_shape), hgrn_new, conv_f_new


def kernel(x_prompt, x_sample, cache_conv_a, state_lru, cache_k, cache_v, state_hgrn, cache_conv_ffn, norm_mix_g, w_in, conv_a_w, conv_a_b, lru_gate_w, lru_gate_b, lru_lambda, hgrn_gamma, hgrn_norm_g, w_branch, w_out, norm_ffn_g, w_up, conv_f_w, conv_f_b, w_down, norm_final_g):
    depth = w_in.shape[0]
    d = x_prompt.shape[-1]
    d_ff = w_down.shape[1]
    assert d == N_HEADS * LANES and d_ff == 3 * d and w_in.shape[2] == (1 + N_P_TILES + N_QKV_TILES) * d
    params = (norm_mix_g, w_in.astype(BF16), conv_a_w, conv_a_b, lru_gate_w, lru_gate_b, lru_lambda, hgrn_gamma,
              hgrn_norm_g, w_branch.astype(BF16), w_out.astype(BF16), norm_ffn_g, w_up.astype(BF16), conv_f_w,
              conv_f_b, w_down.astype(BF16))
    final_g = norm_final_g[None]

    bp = x_prompt.shape[0]
    zeros = lambda *shape: [jnp.zeros(shape, F32)] * depth
    (y_prompt, p_conv_a, p_lru, p_k, p_v, p_hgrn, p_conv_ffn) = _run_group(
        x_prompt, zeros(bp, CONV_A - 1, d), zeros(bp, d), None, None,
        jnp.zeros((depth, bp, N_HEADS, LANES, LANES), F32), zeros(bp, CONV_F - 1, d_ff), params, final_g, depth)
    (y_sample, s_conv_a, s_lru, s_k, s_v, s_hgrn, s_conv_ffn) = _run_group(
        x_sample, cache_conv_a, state_lru, cache_k, cache_v, state_hgrn, cache_conv_ffn, params, final_g, depth)
    return (y_prompt, y_sample, p_conv_a, s_conv_a, p_lru, s_lru, p_k, s_k, p_v, s_v,
            p_hgrn, s_hgrn, p_conv_ffn, s_conv_ffn)
```

```python
import functools

import numpy as np
import jax
import jax.numpy as jnp
from jax import lax
from jax.experimental import pallas as pl
from jax.experimental.pallas import tpu as pltpu

F32 = jnp.float32
BF16 = jnp.bfloat16

EPS = 1e-6
LANES = 128
SUBLANES = 8
VMEM_LIMIT = 48 * 1024 * 1024
PROJ_VMEM_LIMIT = 58 * 1024 * 1024

N_HEADS = 8
LRU_GROUP = 64
C_LRU = 8.0
CONV_A = 4
CONV_F = 3
N_BRANCH = 3
HGRN_CHUNK = 64
HGRN_LEVELS = 6
HGRN_HEAD_GROUP = 4
ATT_BLOCK = 256
ATT_STREAMS = 8
ATT_LOG_CUTOFF = -104.0
ATT_LOG_DEAD = -1e30


def _cparams(semantics):
    return pltpu.CompilerParams(dimension_semantics=semantics, vmem_limit_bytes=VMEM_LIMIT)


def _sigmoid(x):
    return 1.0 / (1.0 + jnp.exp(-x))


def _softplus(x):
    return jnp.maximum(x, 0.0) + jnp.log1p(jnp.exp(-jnp.abs(x)))


def _rms_norm(x, g):
    ms = jnp.mean(x * x, axis=-1, keepdims=True)
    return x * lax.rsqrt(ms + EPS) * g


def _dot(a, b):
    return jnp.dot(a, b, preferred_element_type=F32)


def _dot_nt(a, b):
    return lax.dot_general(a, b, (((1,), (1,)), ((), ())), preferred_element_type=F32)


def _dot_tn(a, b):
    return lax.dot_general(a, b, (((0,), (0,)), ((), ())), preferred_element_type=F32)


def _shift_rows(x, tail, s):
    n_rows = x.shape[0]
    rows = lax.broadcasted_iota(jnp.int32, (SUBLANES, x.shape[1]), 0)
    rolled = pltpu.roll(x, s, 0)
    first = jnp.where(rows < s, pltpu.roll(tail, s, 0), rolled[0:SUBLANES])
    if n_rows == SUBLANES:
        return first
    return jnp.concatenate([first, rolled[SUBLANES:]], axis=0)


def _lru_tile(x, tail, h_prev, cw, cb, wg, gbias, lam, interleave=()):
    pending = list(interleave)

    def emit_other():
        if pending:
            pending.pop(0)()

    tt = x.shape[0]
    emit_other()
    y = cb + cw[CONV_A - 1:CONV_A] * x
    for s in range(1, CONV_A):
        y = y + cw[CONV_A - 1 - s:CONV_A - s] * _shift_rows(x, tail, s)

    gates = _dot(y.astype(BF16), wg) + gbias
    r = _sigmoid(gates[:, :LANES])
    i = _sigmoid(gates[:, LANES:])
    emit_other()
    log_a = (-C_LRU) * r * _softplus(-lam)
    a = jnp.exp(log_a)
    th = jnp.tanh(log_a)
    b = jnp.sqrt(-2.0 * th / (1.0 - th)) * (i * y)

    row = lax.broadcasted_iota(jnp.int32, (tt, LANES), 0)
    b = b + jnp.where(row == 0, a * h_prev, 0.0)
    s = 1
    while s < tt:
        if s in (SUBLANES // 2, 8 * SUBLANES):
            emit_other()
        if s < SUBLANES:
            a_sh = jnp.where(row < s, 1.0, pltpu.roll(a, s, 0))
            b_sh = jnp.where(row < s, 0.0, pltpu.roll(b, s, 0))
        else:
            a_sh = jnp.concatenate([jnp.ones((s, LANES), F32), a[:tt - s]], axis=0)
            b_sh = jnp.concatenate([jnp.zeros((s, LANES), F32), b[:tt - s]], axis=0)
        b = a * b_sh + b
        a = a * a_sh
        s *= 2
    while pending:
        emit_other()
    return b


def _lru_body(x_ref, prev_ref, h0_ref, cw_ref, cb_ref, wg_ref, gb_ref, lam_ref,
              ya_ref, convnew_ref, hlast_ref, tail_scr, h_scr, *, tt):
    t = pl.program_id(2)
    nt = pl.num_programs(2)

    @pl.when(t == 0)
    def _():
        tail_scr[...] = jnp.zeros_like(tail_scr)
        tail_scr[SUBLANES - (CONV_A - 1):SUBLANES, :] = prev_ref[0]
        h_scr[...] = h0_ref[0]

    h = _lru_tile(x_ref[0], tail_scr[...], h_scr[...], cw_ref[...], cb_ref[...], wg_ref[0], gb_ref[0],
                  lam_ref[...])
    tail_scr[...] = x_ref[0, tt - SUBLANES:tt, :]
    h_last = h[tt - 1:tt]
    ya_ref[0] = h.astype(BF16)
    h_scr[...] = h_last

    @pl.when(t == nt - 1)
    def _():
        convnew_ref[0] = x_ref[0, tt - (CONV_A - 1):tt, :]
        hlast_ref[0] = h_last


def _lru(xa3d, conv_prev, h0, conv_w, conv_b, wg, gb, lam):
    bsz, t_len, d_a = xa3d.shape
    nc = d_a // LANES
    tt = min(1024, t_len)
    body = functools.partial(_lru_body, tt=tt)
    return pl.pallas_call(
        body,
        grid=(bsz, nc, t_len // tt),
        in_specs=[
            pl.BlockSpec((1, tt, LANES), lambda b, c, t: (b, t, c)),
            pl.BlockSpec((1, CONV_A - 1, LANES), lambda b, c, t: (b, 0, c)),
            pl.BlockSpec((1, 1, LANES), lambda b, c, t: (b, 0, c)),
            pl.BlockSpec((CONV_A, LANES), lambda b, c, t: (0, c)),
            pl.BlockSpec((1, LANES), lambda b, c, t: (0, c)),
            pl.BlockSpec((1, LANES, 2 * LANES), lambda b, c, t: (c, 0, 0)),
            pl.BlockSpec((1, 1, 2 * LANES), lambda b, c, t: (c, 0, 0)),
            pl.BlockSpec((1, LANES), lambda b, c, t: (0, c)),
        ],
        out_specs=[
            pl.BlockSpec((1, tt, LANES), lambda b, c, t: (b, t, c)),
            pl.BlockSpec((1, CONV_A - 1, LANES), lambda b, c, t: (b, 0, c)),
            pl.BlockSpec((1, 1, LANES), lambda b, c, t: (b, 0, c)),
        ],
        out_shape=[
            jax.ShapeDtypeStruct((bsz, t_len, d_a), BF16),
            jax.ShapeDtypeStruct((bsz, CONV_A - 1, d_a), F32),
            jax.ShapeDtypeStruct((bsz, 1, d_a), F32),
        ],
        scratch_shapes=[pltpu.VMEM((SUBLANES, LANES), F32), pltpu.VMEM((1, LANES), F32)],
        compiler_params=_cparams(("arbitrary", "arbitrary", "arbitrary")),
        name="lru",
    )(xa3d, conv_prev, h0, conv_w, conv_b, wg, gb, lam)


N_P_TILES = 7
N_QKV_TILES = 3
N_STRIPS = 8
N_MM_PIECES = 4
W_RING = 3


def _w_in_tile(j):
    return jnp.where(j == 0, 0, jnp.where(j <= N_P_TILES, j + N_QKV_TILES, j - N_P_TILES))


def _strip(j):
    return jnp.clip(j - 1, 0, N_STRIPS - 1)


def _proj_in_body(*refs, aliased, fused, tm, nt, layer):
    refs = list(refs)
    x_ref, g_ref, w_hbm, cs_ref = refs[:4]
    pos = 6 if aliased else 4
    if fused:
        prev_ref, h0_ref, cw_ref, cb_ref, wg_ref, gb_ref, lam_ref = refs[pos:pos + 7]
        pos += 7
        (p_ref, k_ref, v_ref, qkv_ref, ya_ref, convnew_ref, hlast_ref,
         h_scr, wbuf, wsem, xa_scr, tail_scr, hst_scr) = refs[pos:]
    else:
        p_ref, k_ref, v_ref, qkv_ref, xa_ref, h_scr, wbuf, wsem = refs[pos:]
    i = pl.program_id(0)
    j = pl.program_id(1)
    n_cols = pl.num_programs(1)
    d = wbuf.shape[1]

    step = i * n_cols + j
    n_steps = pl.num_programs(0) * n_cols

    def w_copy(s):
        col = pl.multiple_of(_w_in_tile(s % n_cols) * d, d)
        slot = s % W_RING
        return pltpu.make_async_copy(w_hbm.at[layer, :, pl.ds(col, d)], wbuf.at[slot], wsem.at[slot])

    @pl.when(step == 0)
    def _():
        for s0 in range(W_RING - 1):
            w_copy(jnp.int32(s0)).start()

    @pl.when(step + (W_RING - 1) < n_steps)
    def _():
        w_copy(step + (W_RING - 1)).start()

    w_copy(step).wait()
    w_ref = wbuf.at[step % W_RING]

    @pl.when(j == 0)
    def _():
        h_scr[...] = _rms_norm(x_ref[...], g_ref[...]).astype(BF16)

    def lru_strip(interleave):
        c = j - 1
        t = i % nt

        @pl.when(t == 0)
        def _():
            tail_scr[c] = jnp.zeros((SUBLANES, LANES), F32)
            tail_scr[c, SUBLANES - (CONV_A - 1):SUBLANES, :] = prev_ref[0]
            hst_scr[c] = h0_ref[0]

        h = _lru_tile(xa_scr[c], tail_scr[c], hst_scr[c], cw_ref[...], cb_ref[...], wg_ref[0], gb_ref[0],
                      lam_ref[...], interleave)
        tail_scr[c] = xa_scr[c, tm - SUBLANES:tm, :]
        h_last = h[tm - 1:tm]
        hst_scr[c] = h_last
        ya_ref[...] = h.astype(BF16)

        @pl.when(t == nt - 1)
        def _():
            convnew_ref[0] = xa_scr[c, tm - (CONV_A - 1):tm, :]
            hlast_ref[0] = h_last

    def column_pieces(store):
        width = d // N_MM_PIECES

        def piece(n):
            cols = slice(n * width, (n + 1) * width)
            return lambda: store(cols, _dot(h_scr[...], w_ref[:, cols]))

        return [piece(n) for n in range(N_MM_PIECES)]

    @pl.when(j == 0)
    def _():
        acc = _dot(h_scr[...], w_ref[...])
        if fused:
            for c in range(N_STRIPS):
                xa_scr[c] = acc[:, c * LANES:(c + 1) * LANES]
        else:
            xa_ref[...] = acc

    def store_p(cols, acc):
        p_ref[:, cols] = acc

    def store_q(cols, acc):
        qkv_ref[:, cols] = (acc * cs_ref[:, cols]).astype(BF16)

    @pl.when(jnp.logical_and(j >= 1, j <= N_P_TILES))
    def _():
        if fused:
            lru_strip(column_pieces(store_p))
        else:
            p_ref[...] = _dot(h_scr[...], w_ref[...])

    @pl.when(j == N_P_TILES + 1)
    def _():
        if fused:
            lru_strip(column_pieces(store_q))
        else:
            qkv_ref[...] = (_dot(h_scr[...], w_ref[...]) * cs_ref[...]).astype(BF16)

    for off, dst_ref in ((2, k_ref), (3, v_ref)):
        @pl.when(j == N_P_TILES + off)
        def _(dst_ref=dst_ref):
            acc = _dot(h_scr[...], w_ref[...])
            dst_ref[...] = acc
            qkv_ref[...] = (acc * cs_ref[...]).astype(BF16)


def _proj_in(x2d, g, w_bf16, col_scale, layer, depth, k_all, v_all, t_len, lru_args):
    m, d = x2d.shape
    n_tiles = w_bf16.shape[2] // d
    assert n_tiles == 1 + N_P_TILES + N_QKV_TILES and d == N_STRIPS * LANES
    tm = min(1024, m)
    fused = t_len % tm == 0
    nt = max(t_len // tm, 1)
    bsz = m // t_len
    aliased = k_all is not None
    qkv_tile = lambda j: jnp.clip(j - (N_P_TILES + 1), 0, N_QKV_TILES - 1)
    in_specs = [
        pl.BlockSpec((tm, d), lambda i, j: (i, 0)),
        pl.BlockSpec((1, d), lambda i, j: (0, 0)),
        pl.BlockSpec(memory_space=pl.ANY),
        pl.BlockSpec((1, d), lambda i, j: (0, qkv_tile(j))),
    ]
    args = [x2d, g, w_bf16, col_scale]
    aliases = {}
    if aliased:
        in_specs += [pl.BlockSpec(memory_space=pl.ANY), pl.BlockSpec(memory_space=pl.ANY)]
        args += [k_all, v_all]
        aliases = {4: 1, 5: 2}
    kv_spec = pl.BlockSpec((None, tm, d), lambda i, j: (layer, i, 0))
    out_specs = [
        pl.BlockSpec((tm, d), lambda i, j: (i, jnp.clip(j - 1, 0, N_P_TILES - 1))),
        kv_spec, kv_spec,
        pl.BlockSpec((tm, d), lambda i, j: (i, qkv_tile(j))),
    ]
    out_shape = [
        jax.ShapeDtypeStruct((m, N_P_TILES * d), F32),
        jax.ShapeDtypeStruct((depth, m, d), F32),
        jax.ShapeDtypeStruct((depth, m, d), F32),
        jax.ShapeDtypeStruct((m, N_QKV_TILES * d), BF16),
    ]
    scratch = [pltpu.VMEM((tm, d), BF16), pltpu.VMEM((W_RING, d, d), BF16), pltpu.SemaphoreType.DMA((W_RING,))]
    if fused:
        last = lambda i, j: jnp.where(i % nt == nt - 1, _strip(j), 0)
        in_specs += [
            pl.BlockSpec((1, CONV_A - 1, LANES), lambda i, j: (i // nt, 0, _strip(j))),
            pl.BlockSpec((1, 1, LANES), lambda i, j: (i // nt, 0, _strip(j))),
            pl.BlockSpec((CONV_A, LANES), lambda i, j: (0, _strip(j))),
            pl.BlockSpec((1, LANES), lambda i, j: (0, _strip(j))),
            pl.BlockSpec((1, LANES, 2 * LANES), lambda i, j: (_strip(j), 0, 0)),
            pl.BlockSpec((1, 1, 2 * LANES), lambda i, j: (_strip(j), 0, 0)),
            pl.BlockSpec((1, LANES), lambda i, j: (0, _strip(j))),
        ]
        args += list(lru_args)
        out_specs += [
            pl.BlockSpec((tm, LANES), lambda i, j: (i, _strip(j))),
            pl.BlockSpec((1, CONV_A - 1, LANES), lambda i, j: (i // nt, 0, last(i, j))),
            pl.BlockSpec((1, 1, LANES), lambda i, j: (i // nt, 0, last(i, j))),
        ]
        out_shape += [
            jax.ShapeDtypeStruct((m, d), BF16),
            jax.ShapeDtypeStruct((bsz, CONV_A - 1, d), F32),
            jax.ShapeDtypeStruct((bsz, 1, d), F32),
        ]
        scratch += [pltpu.VMEM((N_STRIPS, tm, LANES), F32), pltpu.VMEM((N_STRIPS, SUBLANES, LANES), F32),
                    pltpu.VMEM((N_STRIPS, 1, LANES), F32)]
    else:
        out_specs.append(pl.BlockSpec((tm, d), lambda i, j: (i, 0)))
        out_shape.append(jax.ShapeDtypeStruct((m, d), F32))
    outs = pl.pallas_call(
        functools.partial(_proj_in_body, aliased=aliased, fused=fused, tm=tm, nt=nt, layer=layer),
        grid=(m // tm, n_tiles),
        in_specs=in_specs,
        out_specs=out_specs,
        out_shape=out_shape,
        scratch_shapes=scratch,
        input_output_aliases=aliases,
        compiler_params=pltpu.CompilerParams(dimension_semantics=("arbitrary", "arbitrary"),
                                             vmem_limit_bytes=PROJ_VMEM_LIMIT),
        name="proj_in",
    )(*args)
    if fused:
        return outs[0], outs[1], outs[2], outs[3], (outs[4], outs[5], outs[6]), None
    return outs[0], outs[1], outs[2], outs[3], None, outs[4]


def _suffix_sum(l, u2):
    hi = l.astype(BF16)
    lo = (l - hi.astype(F32)).astype(BF16)
    return _dot(jnp.concatenate([hi, lo], axis=1), u2)


def _attn_blocks(qs, kvs, u2, accs, carries, mask):
    zs = [_dot_nt(q, kb) for q, (kb, _) in zip(qs, kvs)]
    sps = [jnp.maximum(z, 0.0) + jnp.log(1.0 + jnp.exp(-jnp.abs(z))) for z in zs]
    ls = [-sp if mask is None else jnp.where(mask, -sp, 0.0) for sp in sps]
    sufs = [_suffix_sum(l, u2) for l in ls]
    ws = [jnp.exp(z - sp + (suf + carry)) for z, sp, suf, carry in zip(zs, sps, sufs, carries)]
    if mask is not None:
        ws = [jnp.where(mask, w, 0.0) for w in ws]
    accs = [acc + _dot(w.astype(BF16), vb) for acc, w, (_, vb) in zip(accs, ws, kvs)]
    return accs, [jnp.sum(l, axis=1, keepdims=True) for l in ls]


def _max_log_mass(states, n_left):
    m = None
    for nl, (_, carry) in zip(n_left, states):
        c = jnp.where(nl > 0, carry, ATT_LOG_DEAD)
        m = c if m is None else jnp.maximum(m, c)
    return jnp.max(m)


def _walk_back(qs, states, n_blocks, loads, u2):
    def alive(jj, sts):
        return (_max_log_mass(sts, [nb - jj for nb in n_blocks]) > ATT_LOG_CUTOFF).astype(jnp.int32)

    def cond(st):
        return st[1] > 0

    def body(st):
        jj, _, sts = st
        valids = [nb - 1 - jj >= 0 for nb in n_blocks]
        kvs = [load(jnp.maximum(nb - 1 - jj, 0)) for nb, load in zip(n_blocks, loads)]
        accs, row_sums = _attn_blocks(
            qs, kvs, u2, [acc for acc, _ in sts],
            [jnp.where(valid, carry, ATT_LOG_DEAD) for valid, (_, carry) in zip(valids, sts)], None)
        new = [(acc, jnp.where(valid, carry + row_sum, carry))
               for acc, valid, row_sum, (_, carry) in zip(accs, valids, row_sums, sts)]
        jj = jj + 1
        return jj, alive(jj, new), new

    jj0 = jnp.int32(0)
    return lax.while_loop(cond, body, (jj0, alive(jj0, states), states))[2]


def _diag_blocks(qs, kvs, u2):
    tq = qs[0].shape[0]
    row = lax.broadcasted_iota(jnp.int32, (tq, tq), 0)
    col = lax.broadcasted_iota(jnp.int32, (tq, tq), 1)
    accs, row_sums = _attn_blocks(qs, kvs, u2, [jnp.zeros((tq, LANES), F32)] * len(qs),
                                  [jnp.zeros((tq, 1), F32)] * len(qs), col < row)
    return list(zip(accs, row_sums))


def _attn_fresh_body(q_ref, k_ref, v_ref, u_ref, o_ref, *, tq, n_streams):
    i = pl.program_id(2)
    u2 = u_ref[...]

    def load(j):
        start = pl.multiple_of(j * tq, tq)
        return k_ref[0, pl.ds(start, tq), :], v_ref[0, pl.ds(start, tq), :]

    n_prev = [i * n_streams + s for s in range(n_streams)]
    qs = [q_ref[0, s * tq:(s + 1) * tq, :] for s in range(n_streams)]
    states = _diag_blocks(qs, [load(qi) for qi in n_prev], u2)
    states = _walk_back(qs, states, n_prev, [load] * n_streams, u2)
    for s in range(n_streams):
        o_ref[0, s * tq:(s + 1) * tq, :] = states[s][0].astype(BF16)


def _strict_upper2(n):
    u = np.triu(np.ones((n, n), np.float32), 1).T
    return jnp.asarray(np.concatenate([u, u], axis=0), dtype=BF16)


def _attn_fresh(qkv3d):
    bsz, t_len, d3 = qkv3d.shape
    d_b = d3 // 3
    nh = d_b // LANES
    tq = min(ATT_BLOCK, t_len)
    n_streams = min(ATT_STREAMS, t_len // tq)
    tstep = tq * n_streams
    body = functools.partial(_attn_fresh_body, tq=tq, n_streams=n_streams)
    return pl.pallas_call(
        body,
        grid=(bsz, nh, t_len // tstep),
        in_specs=[
            pl.BlockSpec((1, tstep, LANES), lambda b, h, i: (b, i, h)),
            pl.BlockSpec((1, t_len, LANES), lambda b, h, i: (b, 0, nh + h)),
            pl.BlockSpec((1, t_len, LANES), lambda b, h, i: (b, 0, 2 * nh + h)),
            pl.BlockSpec((2 * tq, tq), lambda b, h, i: (0, 0)),
        ],
        out_specs=pl.BlockSpec((1, tstep, LANES), lambda b, h, i: (b, i, h)),
        out_shape=jax.ShapeDtypeStruct((bsz, t_len, d_b), BF16),
        compiler_params=_cparams(("arbitrary", "arbitrary", "arbitrary")),
        name="attn_fresh",
    )(qkv3d, qkv3d, qkv3d, _strict_upper2(tq))


def _attn_cached_body(q_ref, k_ref, v_ref, u_ref, kp_hbm, vp_hbm, up_ref, o_ref, kbuf, vbuf, sem,
                      *, layer, tkp, n_past):
    b = pl.program_id(0)
    heads = range(N_HEADS)
    cols = [slice(h * LANES, (h + 1) * LANES) for h in heads]
    qs = [q_ref[0, :, c] for c in cols]
    states = _diag_blocks(qs, [(k_ref[0, :, c], v_ref[0, :, c]) for c in cols], u_ref[...])
    up2 = up_ref[...]
    block_rows = tkp * N_HEADS

    def block_copies(j):
        src = pl.ds(pl.multiple_of(j * block_rows, block_rows), block_rows)
        return (pltpu.make_async_copy(kp_hbm.at[layer, b, src], kbuf, sem.at[0]),
                pltpu.make_async_copy(vp_hbm.at[layer, b, src], vbuf, sem.at[1]))

    def alive(jj, sts):
        return (_max_log_mass(sts, [n_past - jj] * N_HEADS) > ATT_LOG_CUTOFF).astype(jnp.int32)

    def cond(st):
        return st[1] > 0

    def body(st):
        jj, _, sts = st
        copy_k, copy_v = block_copies(n_past - 1 - jj)
        copy_k.start()
        copy_v.start()
        copy_k.wait()
        copy_v.wait()
        kvs = [(kbuf[pl.ds(h, tkp, stride=N_HEADS), :].astype(BF16),
                vbuf[pl.ds(h, tkp, stride=N_HEADS), :].astype(BF16)) for h in heads]
        accs, row_sums = _attn_blocks(qs, kvs, up2, [acc for acc, _ in sts], [carry for _, carry in sts], None)
        new = [(acc, carry + row_sum) for acc, row_sum, (_, carry) in zip(accs, row_sums, sts)]
        jj = jj + 1
        return jj, alive(jj, new), new

    jj0 = jnp.int32(0)
    states = lax.while_loop(cond, body, (jj0, alive(jj0, states), states))[2]
    for h in heads:
        o_ref[0, :, cols[h]] = states[h][0].astype(BF16)


def _attn_cached(qkv3d, k_cache, v_cache, layer):
    bsz, t_len, d3 = qkv3d.shape
    d = d3 // 3
    depth, _, past, nh, dh = k_cache.shape
    assert nh == N_HEADS and dh == LANES and d == nh * dh and t_len <= ATT_BLOCK
    tkp = min(ATT_BLOCK, past)
    kc = k_cache.reshape(depth, bsz, past * nh, dh)
    vc = v_cache.reshape(depth, bsz, past * nh, dh)
    new_spec = lambda tile: pl.BlockSpec((1, t_len, d), lambda b: (b, 0, tile))
    body = functools.partial(_attn_cached_body, layer=layer, tkp=tkp, n_past=past // tkp)
    return pl.pallas_call(
        body,
        grid=(bsz,),
        in_specs=[
            new_spec(0), new_spec(1), new_spec(2),
            pl.BlockSpec((2 * t_len, t_len), lambda b: (0, 0)),
            pl.BlockSpec(memory_space=pl.ANY), pl.BlockSpec(memory_space=pl.ANY),
            pl.BlockSpec((2 * tkp, tkp), lambda b: (0, 0)),
        ],
        out_specs=pl.BlockSpec((1, t_len, d), lambda b: (b, 0, 0)),
        out_shape=jax.ShapeDtypeStruct((bsz, t_len, d), BF16),
        scratch_shapes=[pltpu.VMEM((tkp * nh, dh), F32), pltpu.VMEM((tkp * nh, dh), F32),
                        pltpu.SemaphoreType.DMA((2,))],
        compiler_params=_cparams(("arbitrary",)),
        name="attn_cached",
    )(qkv3d, qkv3d, qkv3d, _strict_upper2(t_len), kc, vc, _strict_upper2(tkp))


def _hgrn_tables():
    c = HGRN_CHUNK
    n_seg = 2 + HGRN_LEVELS
    d = np.zeros((n_seg * c, c), np.float32)
    jj = np.arange(c)
    amask = np.zeros((HGRN_LEVELS + 1, c, c), np.float32)
    amask[HGRN_LEVELS] = np.eye(c)
    for r in range(c):
        d[r] = jj <= r
        d[c + r] = jj > r
    for li in range(HGRN_LEVELS):
        m = c >> (li + 1)
        for r in range(c):
            blk, pos = divmod(r, 2 * m)
            boundary = blk * 2 * m + m - 1
            if pos >= m:
                d[(2 + li) * c + r] = (jj > boundary) & (jj <= r)
            else:
                d[(2 + li) * c + r] = (jj > r) & (jj <= boundary)
        late = (jj % (2 * m)) >= m
        blk_id = jj // (2 * m)
        amask[li] = (blk_id[:, None] == blk_id[None, :]) & late[:, None] & ~late[None, :]
    d3 = np.concatenate([d, d, d], axis=1)
    gmask = np.stack([np.kron(np.eye(HGRN_HEAD_GROUP, dtype=np.float32), a) for a in amask])
    return jnp.asarray(d3, dtype=BF16), jnp.asarray(gmask)


def _hgrn_chunk_heads(qr, fr, v, gr, sts, lb, lb_is_zero, ng, d3, amask_ref):
    c = HGRN_CHUNK
    q = qr * _sigmoid(qr)
    e = jnp.exp(-jnp.abs(fr))
    r = 1.0 / (1.0 + e)
    pos = fr >= 0.0
    kk = (1.0 - lb) * jnp.where(pos, e * r, r)
    if lb_is_zero:
        lf = jnp.minimum(fr, 0.0) - jnp.log(1.0 + e)
    else:
        lf = jnp.log(lb + (1.0 - lb) * jnp.where(pos, r, e * r))

    hi = lf.astype(BF16)
    r1 = lf - hi.astype(F32)
    mid = r1.astype(BF16)
    lo = (r1 - mid.astype(F32)).astype(BF16)
    ex = jnp.exp(_dot(d3, jnp.concatenate([hi, mid, lo], axis=0)))
    xb = ex[0:c]
    qin = (q * xb).astype(BF16)
    kr = (kk * ex[c:2 * c]).astype(BF16)
    qb, kb, vb = q.astype(BF16), kk.astype(BF16), v.astype(BF16)
    qms = [(q * ex[(2 + li) * c:(3 + li) * c]).astype(BF16) for li in range(HGRN_LEVELS)]
    kms = [(kk * ex[(2 + li) * c:(3 + li) * c]).astype(BF16) for li in range(HGRN_LEVELS)]
    gate = gr * _sigmoid(gr)

    def rows(x):
        return jnp.concatenate([x[:, h * LANES:(h + 1) * LANES] for h in range(len(sts))], axis=0)

    att = amask_ref[HGRN_LEVELS] * _dot_nt(rows(qb), rows(kb))
    for li in range(HGRN_LEVELS):
        att = att + amask_ref[li] * _dot_nt(rows(qms[li]), rows(kms[li]))
    o_intra = _dot(att.astype(BF16), rows(vb))

    outs = []
    for h, st in enumerate(sts):
        cols = slice(h * LANES, (h + 1) * LANES)
        o = _dot_nt(qin[:, cols], st.astype(BF16)) + o_intra[h * c:(h + 1) * c]
        st_new = st * xb[c - 1:c, cols] + _dot_tn(vb[:, cols], kr[:, cols])
        outs.append((_rms_norm(o, ng) * gate[:, cols], st_new))
    return outs


def _hgrn_body(q_ref, f_ref, i_ref, g_ref, s0_ref, gamma_ref, ng_ref, d3_ref, amask_ref,
               y_ref, slast_ref, st_scr, *, tt, layer):
    t = pl.program_id(1)
    nt = pl.num_programs(1)
    c = HGRN_CHUNK

    @pl.when(t == 0)
    def _():
        for h in range(N_HEADS):
            st_scr[h] = s0_ref[0, h].T

    gam = gamma_ref[...]
    e = jnp.exp(gam - jnp.max(gam, axis=0, keepdims=True))
    lb_all = jnp.sum(e[1:layer + 1], axis=0, keepdims=True) / jnp.sum(e, axis=0, keepdims=True) \
        if layer > 0 else jnp.zeros((1, N_HEADS * LANES), F32)
    ng = ng_ref[...]
    d3 = d3_ref[...]

    def chunk(ci, carry):
        r0 = pl.multiple_of(ci * c, c)
        for h0 in range(0, N_HEADS, HGRN_HEAD_GROUP):
            heads = range(h0, h0 + HGRN_HEAD_GROUP)
            cols = slice(h0 * LANES, (h0 + HGRN_HEAD_GROUP) * LANES)
            outs = _hgrn_chunk_heads(
                q_ref[0, pl.ds(r0, c), cols], f_ref[0, pl.ds(r0, c), cols],
                i_ref[0, pl.ds(r0, c), cols], g_ref[0, pl.ds(r0, c), cols],
                [st_scr[h] for h in heads], lb_all[:, cols], layer == 0, ng, d3, amask_ref)
            for h, (y, st_new) in zip(heads, outs):
                st_scr[h] = st_new
                y_ref[0, pl.ds(r0, c), h * LANES:(h + 1) * LANES] = y.astype(BF16)
        return carry

    lax.fori_loop(0, tt // c, chunk, 0, unroll=True)

    @pl.when(t == nt - 1)
    def _():
        for h in range(N_HEADS):
            slast_ref[0, h] = st_scr[h].T


def _hgrn(p3d, s0_all, gamma, norm_g, layer, tile0):
    bsz, t_len, _ = p3d.shape
    nh = N_HEADS
    d = nh * LANES
    depth = gamma.shape[0]
    tt = min(512, t_len)
    d3, amask = _hgrn_tables()
    body = functools.partial(_hgrn_body, tt=tt, layer=layer)

    def col_spec(off):
        return pl.BlockSpec((1, tt, d), lambda b, t: (b, t, tile0 + off))

    def const_spec(shape):
        return pl.BlockSpec(shape, lambda b, t: (0,) * len(shape))

    return pl.pallas_call(
        body,
        grid=(bsz, t_len // tt),
        in_specs=[
            col_spec(0), col_spec(1), col_spec(2), col_spec(3),
            pl.BlockSpec((None, 1, nh, LANES, LANES), lambda b, t: (layer, b, 0, 0, 0)),
            const_spec((depth, d)),
            const_spec((1, LANES)),
            const_spec(d3.shape), const_spec(amask.shape),
        ],
        out_specs=[
            pl.BlockSpec((1, tt, d), lambda b, t: (b, t, 0)),
            pl.BlockSpec((1, nh, LANES, LANES), lambda b, t: (b, 0, 0, 0)),
        ],
        out_shape=[
            jax.ShapeDtypeStruct((bsz, t_len, d), BF16),
            jax.ShapeDtypeStruct((bsz, nh, LANES, LANES), F32),
        ],
        scratch_shapes=[pltpu.VMEM((nh, LANES, LANES), F32)],
        compiler_params=_cparams(("arbitrary", "arbitrary")),
        name="hgrn",
    )(p3d, p3d, p3d, p3d, s0_all, gamma, norm_g, d3, amask)


def _merge_body(x_ref, ya_ref, yb_ref, yc_ref, ga_ref, gb_ref, gc_ref, wb_ref, wo_ref, o_ref):
    merged = _sigmoid(ga_ref[...]) * _dot(ya_ref[...], wb_ref[0])
    merged = merged + _sigmoid(gb_ref[...]) * _dot(yb_ref[...], wb_ref[1])
    merged = merged + _sigmoid(gc_ref[...]) * _dot(yc_ref[...], wb_ref[2])
    o_ref[...] = x_ref[...] + _dot(merged.astype(BF16), wo_ref[...])


def _merge(x2d, ya, yb, yc, p2d, w_branch, w_out, layer, gtile0):
    m, d = x2d.shape
    tm = min(512, m)
    row = lambda i: (i, 0)
    return pl.pallas_call(
        _merge_body,
        grid=(m // tm,),
        in_specs=[
            pl.BlockSpec((tm, d), row),
            pl.BlockSpec((tm, d), row), pl.BlockSpec((tm, d), row), pl.BlockSpec((tm, d), row),
            pl.BlockSpec((tm, d), lambda i: (i, gtile0)),
            pl.BlockSpec((tm, d), lambda i: (i, gtile0 + 1)),
            pl.BlockSpec((tm, d), lambda i: (i, gtile0 + 2)),
            pl.BlockSpec((None, N_BRANCH, d, d), lambda i: (layer, 0, 0, 0), pipeline_mode=pl.Buffered(1)),
            pl.BlockSpec((None, d, d), lambda i: (layer, 0, 0), pipeline_mode=pl.Buffered(1)),
        ],
        out_specs=pl.BlockSpec((tm, d), row),
        out_shape=jax.ShapeDtypeStruct((m, d), F32),
        compiler_params=_cparams(("arbitrary",)),
        name="merge",
    )(x2d, ya, yb, yc, p2d, p2d, p2d, w_branch, w_out)


def _gelu_tanh(x):
    return 0.5 * x * (1.0 + jnp.tanh(np.sqrt(2.0 / np.pi).astype(np.float32) * (x + 0.044715 * (x * x * x))))


def _ffn_body(*refs, nb, tm, final):
    if final:
        (x_ref, g_ref, wu_ref, wv_ref, wd_ref, cw_ref, cb_ref, prev_ref, gf_ref,
         o_ref, convnew_ref, y_ref, h_scr, acc_scr, tail_scr) = refs
    else:
        (x_ref, g_ref, wu_ref, wv_ref, wd_ref, cw_ref, cb_ref, prev_ref,
         o_ref, convnew_ref, h_scr, acc_scr, tail_scr) = refs
    t = pl.program_id(1)
    nt = pl.num_programs(1)
    k = pl.program_id(2)
    nk = pl.num_programs(2)
    d = x_ref.shape[-1]

    @pl.when(k == 0)
    def _():
        h_scr[...] = _rms_norm(x_ref[...].reshape(nb * tm, d), g_ref[...]).astype(BF16)
        acc_scr[...] = jnp.zeros_like(acc_scr)

    @pl.when(t == 0)
    def _():
        tail_scr[k] = jnp.zeros(tail_scr.shape[1:], F32)
        tail_scr[k, :, SUBLANES - (CONV_F - 1):SUBLANES, :] = prev_ref[...]

    h = h_scr[...]
    u = _dot(h, wu_ref[...])
    v = _dot(h, wv_ref[...])
    ucs = []
    for bi in range(nb):
        ub = u[bi * tm:(bi + 1) * tm]
        tail = tail_scr[k, bi]
        uc = cb_ref[...] + cw_ref[CONV_F - 1:CONV_F, :] * ub
        for s in range(1, CONV_F):
            uc = uc + cw_ref[CONV_F - 1 - s:CONV_F - s, :] * _shift_rows(ub, tail, s)
        ucs.append(uc)
        tail_scr[k, bi] = ub[tm - SUBLANES:tm]
    uc = ucs[0] if nb == 1 else jnp.concatenate(ucs, axis=0)
    acc_scr[...] += _dot((_gelu_tanh(uc) * v).astype(BF16), wd_ref[...])

    @pl.when(t == nt - 1)
    def _():
        for bi in range(nb):
            convnew_ref[bi] = u[(bi + 1) * tm - (CONV_F - 1):(bi + 1) * tm]

    @pl.when(k == nk - 1)
    def _():
        x_new = x_ref[...].reshape(nb * tm, d) + acc_scr[...]
        o_ref[...] = x_new.reshape(nb, tm, d)
        if final:
            y_ref[...] = _rms_norm(x_new, gf_ref[...]).reshape(nb, tm, d)


def _ffn(x3d, g, w_up, w_down, layer, conv_w, conv_b, conv_prev, final_g):
    bsz, t_len, d = x3d.shape
    d_ff = w_down.shape[1]
    tm = min(1024, t_len)
    nt = t_len // tm
    nb = min(bsz, 1024 // tm) if nt == 1 else 1
    assert bsz % nb == 0
    kc = 512
    nk = d_ff // kc
    final = final_g is not None
    body = functools.partial(_ffn_body, nb=nb, tm=tm, final=final)
    xspec = pl.BlockSpec((nb, tm, d), lambda b, t, k: (b, t, 0))
    in_specs = [
        xspec,
        pl.BlockSpec((1, d), lambda b, t, k: (0, 0)),
        pl.BlockSpec((None, d, kc), lambda b, t, k: (layer, 0, k)),
        pl.BlockSpec((None, d, kc), lambda b, t, k: (layer, 0, nk + k)),
        pl.BlockSpec((None, kc, d), lambda b, t, k: (layer, k, 0)),
        pl.BlockSpec((CONV_F, kc), lambda b, t, k: (0, k)),
        pl.BlockSpec((1, kc), lambda b, t, k: (0, k)),
        pl.BlockSpec((nb, CONV_F - 1, kc), lambda b, t, k: (b, 0, k)),
    ]
    args = [x3d, g, w_up, w_up, w_down, conv_w, conv_b, conv_prev]
    convnew_spec = pl.BlockSpec((nb, CONV_F - 1, kc), lambda b, t, k: (b, 0, jnp.where(t == nt - 1, k, 0)))
    out_specs = [xspec, convnew_spec]
    out_shape = [jax.ShapeDtypeStruct((bsz, t_len, d), F32), jax.ShapeDtypeStruct((bsz, CONV_F - 1, d_ff), F32)]
    if final:
        in_specs.append(pl.BlockSpec((1, d), lambda b, t, k: (0, 0)))
        args.append(final_g)
        out_specs.append(xspec)
        out_shape.append(jax.ShapeDtypeStruct((bsz, t_len, d), F32))
    return pl.pallas_call(
        body,
        grid=(bsz // nb, nt, nk),
        in_specs=in_specs,
        out_specs=out_specs,
        out_shape=out_shape,
        scratch_shapes=[pltpu.VMEM((nb * tm, d), BF16), pltpu.VMEM((nb * tm, d), F32),
                        pltpu.VMEM((nk, nb, SUBLANES, kc), F32)],
        compiler_params=_cparams(("arbitrary", "arbitrary", "arbitrary")),
        name="ffn",
    )(*args)


def _lru_gate_weights(gate_w, gate_b):
    n_groups = gate_w.shape[1]
    per = LANES // LRU_GROUP
    nc = n_groups // per
    w = gate_w.reshape(2, nc, per, LRU_GROUP, LRU_GROUP)
    eye = jnp.eye(per, dtype=gate_w.dtype)
    wg = jnp.einsum('gcpij,pq->cpigqj', w, eye).reshape(nc, LANES, 2 * LANES)
    gb = gate_b.reshape(2, nc, LANES).transpose(1, 0, 2).reshape(nc, 1, 2 * LANES)
    return wg.astype(BF16), gb


def _layer(x3d, conv_a_prev, h0, k_cache, v_cache, s0_all, conv_f_prev, layer, depth, k_all, v_all, final_g,
           norm_mix_g, w_in, conv_a_w, conv_a_b, lru_gate_w, lru_gate_b, lru_lambda, hgrn_gamma,
           hgrn_norm_g, w_branch, w_out, norm_ffn_g, w_up, conv_f_w, conv_f_b, w_down):
    bsz, t_len, d = x3d.shape
    m = bsz * t_len
    x2d = x3d.reshape(m, d)
    col_scale = jnp.concatenate([jnp.full((1, d), LANES ** -0.5, F32), jnp.ones((1, 2 * d), F32)], axis=1)
    wg, gb = _lru_gate_weights(lru_gate_w[layer], lru_gate_b[layer])
    lru_args = (conv_a_prev, h0[:, None, :], conv_a_w[layer], conv_a_b[layer][None], wg, gb, lru_lambda[layer][None])
    p2d, k_all, v_all, qkv, lru_out, xa = _proj_in(x2d, norm_mix_g[layer][None], w_in, col_scale,
                                                   layer, depth, k_all, v_all, t_len, lru_args)
    p3d = p2d.reshape(bsz, t_len, -1)
    if lru_out is None:
        lru_out = _lru(xa.reshape(bsz, t_len, d), *lru_args)
    ya, conv_a_new, h_last = lru_out

    qkv3d = qkv.reshape(bsz, t_len, N_QKV_TILES * d)
    yb = _attn_fresh(qkv3d) if k_cache is None else _attn_cached(qkv3d, k_cache, v_cache, layer)

    yc, s_last = _hgrn(p3d, s0_all, hgrn_gamma, hgrn_norm_g[layer][None], layer, 0)

    x1 = _merge(x2d, ya.reshape(m, d), yb.reshape(m, d), yc.reshape(m, d), p2d,
                w_branch, w_out, layer, N_P_TILES - N_BRANCH)

    outs = _ffn(x1.reshape(bsz, t_len, d), norm_ffn_g[layer][None], w_up, w_down, layer,
                conv_f_w[layer], conv_f_b[layer][None], conv_f_prev, final_g)
    x_new, conv_f_new = outs[0], outs[1]
    y = outs[2] if final_g is not None else None
    return x_new, y, k_all, v_all, (conv_a_new, h_last[:, 0, :], s_last, conv_f_new)


def _run_group(x, conv_a, lru, k_cache, v_cache, hgrn, conv_f, params, final_g, depth):
    bsz, t_len, d = x.shape
    k_all = v_all = y = None
    states = []
    for l in range(depth):
        x, y, k_all, v_all, st = _layer(
            x, conv_a[l], lru[l], k_cache, v_cache, hgrn, conv_f[l], l, depth, k_all, v_all,
            final_g if l == depth - 1 else None, *params)
        states.append(st)
    conv_a_new, lru_new, hgrn_new, conv_f_new = (jnp.stack(z) for z in zip(*states))
    kv_shape = (depth, bsz, t_len, N_HEADS, LANES)
    return y, conv_a_new, lru_new, k_all.reshape(kv_shape), v_all.reshape(kv_shape), hgrn_new, conv_f_new


def kernel(x_prompt, x_sample, cache_conv_a, state_lru, cache_k, cache_v, state_hgrn, cache_conv_ffn, norm_mix_g, w_in, conv_a_w, conv_a_b, lru_gate_w, lru_gate_b, lru_lambda, hgrn_gamma, hgrn_norm_g, w_branch, w_out, norm_ffn_g, w_up, conv_f_w, conv_f_b, w_down, norm_final_g):
    depth = w_in.shape[0]
    d = x_prompt.shape[-1]
    d_ff = w_down.shape[1]
    assert d == N_HEADS * LANES and d_ff == 3 * d and w_in.shape[2] == (1 + N_P_TILES + N_QKV_TILES) * d
    params = (norm_mix_g, w_in.astype(BF16), conv_a_w, conv_a_b, lru_gate_w, lru_gate_b, lru_lambda, hgrn_gamma,
              hgrn_norm_g, w_branch.astype(BF16), w_out.astype(BF16), norm_ffn_g, w_up.astype(BF16), conv_f_w,
              conv_f_b, w_down.astype(BF16))
    final_g = norm_final_g[None]

    bp = x_prompt.shape[0]
    zeros = lambda *shape: [jnp.zeros(shape, F32)] * depth
    (y_prompt, p_conv_a, p_lru, p_k, p_v, p_hgrn, p_conv_ffn) = _run_group(
        x_prompt, zeros(bp, CONV_A - 1, d), zeros(bp, d), None, None,
        jnp.zeros((depth, bp, N_HEADS, LANES, LANES), F32), zeros(bp, CONV_F - 1, d_ff), params, final_g, depth)
    (y_sample, s_conv_a, s_lru, s_k, s_v, s_hgrn, s_conv_ffn) = _run_group(
        x_sample, cache_conv_a, state_lru, cache_k, cache_v, state_hgrn, cache_conv_ffn, params, final_g, depth)
    return (y_prompt, y_sample, p_conv_a, s_conv_a, p_lru, s_lru, p_k, s_k, p_v, s_v,
            p_hgrn, s_hgrn, p_conv_ffn, s_conv_ffn)
```
